```python
import jax, jax.numpy as jnp
from jax import lax
import numpy as np

D_MODEL = 1024
BATCH = 16
SEQ = 2048
DEPTH = 4

GRID_W = 64
CTX_LEN = 256
HEAD_DIM = 64
BRANCH_W = D_MODEL // 4
MIX_W = 4 * BRANCH_W
H_A = BRANCH_W // HEAD_DIM
G_D = BRANCH_W // HEAD_DIM
QKV_CONV = 3
SHORT_CONV = 3
CONF_CONV = 31
GDN_CHUNK = 64
GDN_CHUNK_LOG2 = 6
MLP_CHUNK = 128
EPS = 1e-6

A_COLS = 3 * BRANCH_W + 4 * H_A
B_COLS = 3 * BRANCH_W
C_COLS = 2 * BRANCH_W
D_COLS = 2 * BRANCH_W
IN_COLS = A_COLS + B_COLS + C_COLS + D_COLS + MIX_W

kernel_name = 'hybrid_parallel_group_flow_block'


def rmsnorm(x, g):
    xf = x.astype(jnp.float32)
    y = xf * lax.rsqrt(jnp.mean(xf * xf, axis=-1, keepdims=True) + EPS)
    return (y * g.astype(jnp.float32)).astype(x.dtype)


def layernorm(x, g, b):
    xf = x.astype(jnp.float32)
    mu = jnp.mean(xf, axis=-1, keepdims=True)
    xc = xf - mu
    var = jnp.mean(xc * xc, axis=-1, keepdims=True)
    y = xc * lax.rsqrt(var + EPS) * g.astype(jnp.float32) + b.astype(jnp.float32)
    return y.astype(x.dtype)


def l2norm(x):
    return x * lax.rsqrt(jnp.sum(x * x, axis=-1, keepdims=True) + EPS)


def dwconv(x, w):
    k = w.shape[0]
    return lax.conv_general_dilated(
        x, w[:, None, :].astype(x.dtype), window_strides=(1,),
        padding=[(k // 2, k // 2)], dimension_numbers=('NWC', 'WIO', 'NWC'),
        feature_group_count=x.shape[-1])


def to_col_major(x):
    bsz, t, ch = x.shape
    rows = t // GRID_W
    return x.reshape(bsz, rows, GRID_W, ch).transpose(0, 2, 1, 3).reshape(bsz, t, ch)


def from_col_major(x):
    bsz, t, ch = x.shape
    rows = t // GRID_W
    return x.reshape(bsz, GRID_W, rows, ch).transpose(0, 2, 1, 3).reshape(bsz, t, ch)


def gdn_chunked(q, k, v, g, beta, s0):
    bsz, t, h, dk = q.shape
    n = t // GDN_CHUNK

    def chunks(a):
        a = a.reshape((bsz, n, GDN_CHUNK) + a.shape[2:])
        return jnp.moveaxis(a, 3, 2)

    qc, kc, vc, gc, bc = (chunks(a) for a in (q, k, v, g, beta))
    gcum = jnp.cumsum(gc, axis=-1)
    idx = jnp.arange(GDN_CHUNK)
    incl = idx[:, None] >= idx[None, :]
    strict = idx[:, None] > idx[None, :]
    decay = jnp.exp(jnp.where(incl, gcum[..., :, None] - gcum[..., None, :], -jnp.inf))
    kb = kc * bc[..., None]
    lmat = -jnp.where(strict, jnp.einsum('bnhid,bnhjd->bnhij', kb, kc) * decay, 0.0)
    tinv = jnp.eye(GDN_CHUNK, dtype=jnp.float32) + lmat
    power = lmat
    for _ in range(GDN_CHUNK_LOG2 - 1):
        power = power @ power
        tinv = tinv + tinv @ power
    u = tinv @ (vc * bc[..., None])
    w = tinv @ (kb * jnp.exp(gcum)[..., None])
    attn = jnp.einsum('bnhid,bnhjd->bnhij', qc, kc) * decay
    qg = qc * jnp.exp(gcum)[..., None]
    kdec = kc * jnp.exp(gcum[..., -1:] - gcum)[..., None]
    glast = jnp.exp(gcum[..., -1])

    def step(s, xs):
        qg_i, kdec_i, u_i, w_i, attn_i, gl_i = xs
        v_new = u_i - w_i @ s
        o_i = qg_i @ s + attn_i @ v_new
        s = s * gl_i[..., None, None] + jnp.einsum('bhcd,bhce->bhde', kdec_i, v_new)
        return s, o_i

    xs = tuple(jnp.moveaxis(a, 1, 0) for a in (qg, kdec, u, w, attn, glast))
    s_fin, o = lax.scan(step, s0, xs)
    o = o.transpose(1, 0, 3, 2, 4).reshape(bsz, t, h, -1)
    return o, s_fin


def gdn_bidir(pa, s0_f, s0_b, conv_w, a_log, dt_bias):
    bsz, t, _ = pa.shape
    qkv = jax.nn.silu(dwconv(pa[..., :3 * BRANCH_W], conv_w)).astype(jnp.float32)
    q, k, v = (a.reshape(bsz, t, H_A, HEAD_DIM) for a in jnp.split(qkv, 3, axis=-1))
    q = l2norm(q) * (HEAD_DIM ** -0.5)
    k = l2norm(k)
    ab = pa[..., 3 * BRANCH_W:A_COLS].astype(jnp.float32).reshape(bsz, t, 4, H_A)
    g = -jnp.exp(a_log.astype(jnp.float32)) * jax.nn.softplus(ab[:, :, :2] + dt_bias.astype(jnp.float32))
    beta = jax.nn.sigmoid(ab[:, :, 2:])
    o_f, s_f = gdn_chunked(q, k, v, g[:, :, 0], beta[:, :, 0], s0_f)
    rev = lambda a: jnp.flip(a, axis=1)
    o_b, s_b = gdn_chunked(rev(q), rev(k), rev(v), rev(g[:, :, 1]), rev(beta[:, :, 1]), s0_b)
    return (o_f + rev(o_b)).astype(pa.dtype), s_f, s_b


def spatial_gate(u, v, w_s, b_s):
    bsz, t, _ = v.shape
    n = t // MLP_CHUNK
    vv = v.reshape(bsz, n, MLP_CHUNK, G_D, HEAD_DIM)
    mixed = jnp.einsum('gpq,bnqgd->bnpgd', w_s, vv) + b_s.T[None, None, :, :, None]
    return u * mixed.reshape(bsz, t, BRANCH_W)


def mixer_output(o_a, p, gdn_norm_g, short_conv_w, conf_conv_w, conf_conv_b, conf_ln_g,
                 conf_ln_b, smlp_ln_g, smlp_ln_b, smlp_w, smlp_b, w_out):
    bsz, t, _ = p.shape
    y_a = rmsnorm(o_a, gdn_norm_g).reshape(bsz, t, BRANCH_W)
    off = A_COLS
    b_gate, c_gate, x_in = jnp.split(p[..., off:off + B_COLS], 3, axis=-1)
    y_b = b_gate * dwconv(c_gate * x_in, short_conv_w)
    off += B_COLS
    glu_a, glu_b = jnp.split(p[..., off:off + C_COLS], 2, axis=-1)
    z = dwconv(glu_a * jax.nn.sigmoid(glu_b), conf_conv_w) + conf_conv_b
    y_c = jax.nn.silu(layernorm(z, conf_ln_g, conf_ln_b))
    off += C_COLS
    u, v = jnp.split(p[..., off:off + D_COLS], 2, axis=-1)
    y_d = spatial_gate(u, layernorm(v, smlp_ln_g, smlp_ln_b), smlp_w, smlp_b)
    y = jnp.concatenate([y_a, y_b, y_c, y_d], axis=-1) * jax.nn.silu(p[..., IN_COLS - MIX_W:])
    return y @ w_out


def setup_inputs(seed: int = 0) -> dict:
    key = jax.random.key(seed)
    ks = jax.random.split(key, 24)
    nrm = lambda k, s, sc: jax.random.normal(k, s, jnp.float32) * sc
    dt = jnp.exp(jax.random.uniform(ks[9], (DEPTH, 2, H_A), jnp.float32, np.log(1e-3), np.log(1e-1)))
    return {
        'x': nrm(ks[0], (BATCH, SEQ, D_MODEL), 1.0),
        'c': nrm(ks[1], (BATCH, D_MODEL), 1.0),
        'ctx': nrm(ks[2], (BATCH, CTX_LEN, D_MODEL), 1.0),
        'c_ctx': nrm(ks[3], (D_MODEL,), 1.0),
        'norm_g': 1.0 + nrm(ks[4], (DEPTH, D_MODEL), 0.05),
        'w_ada': nrm(ks[5], (DEPTH, D_MODEL, 3 * D_MODEL), 0.5 * D_MODEL ** -0.5),
        'b_ada': nrm(ks[6], (DEPTH, 3 * D_MODEL), 0.02),
        'w_in': nrm(ks[7], (DEPTH, D_MODEL, IN_COLS), D_MODEL ** -0.5),
        'qkv_conv_w': nrm(ks[8], (DEPTH, QKV_CONV, 3 * BRANCH_W), QKV_CONV ** -0.5),
        'a_log': jnp.log(jax.random.uniform(ks[10], (DEPTH, 2, H_A), jnp.float32, 1.0, 16.0)),
        'dt_bias': dt + jnp.log(-jnp.expm1(-dt)),
        'gdn_norm_g': 1.0 + nrm(ks[11], (DEPTH, HEAD_DIM), 0.05),
        'short_conv_w': nrm(ks[12], (DEPTH, SHORT_CONV, BRANCH_W), SHORT_CONV ** -0.5),
        'conf_conv_w': nrm(ks[13], (DEPTH, CONF_CONV, BRANCH_W), CONF_CONV ** -0.5),
        'conf_conv_b': nrm(ks[14], (DEPTH, BRANCH_W), 0.02),
        'conf_ln_g': 1.0 + nrm(ks[15], (DEPTH, BRANCH_W), 0.05),
        'conf_ln_b': nrm(ks[16], (DEPTH, BRANCH_W), 0.02),
        'smlp_ln_g': 1.0 + nrm(ks[17], (DEPTH, BRANCH_W), 0.05),
        'smlp_ln_b': nrm(ks[18], (DEPTH, BRANCH_W), 0.02),
        'smlp_w': nrm(ks[19], (DEPTH, G_D, MLP_CHUNK, MLP_CHUNK), MLP_CHUNK ** -0.5),
        'smlp_b': 1.0 + nrm(ks[20], (DEPTH, G_D, MLP_CHUNK), 0.02),
        'w_out': nrm(ks[21], (DEPTH, MIX_W, D_MODEL), MIX_W ** -0.5),
        'final_g': 1.0 + nrm(ks[22], (D_MODEL,), 0.05),
    }


def reference(x, c, ctx, c_ctx, norm_g, w_ada, b_ada, w_in, qkv_conv_w, a_log, dt_bias,
              gdn_norm_g, short_conv_w, conf_conv_w, conf_conv_b, conf_ln_g, conf_ln_b,
              smlp_ln_g, smlp_ln_b, smlp_w, smlp_b, w_out, final_g):
    bsz = x.shape[0]
    xc = ctx
    silu_c = jax.nn.silu(c)
    silu_cc = jax.nn.silu(c_ctx)
    s_zero = jnp.zeros((bsz, H_A, HEAD_DIM, HEAD_DIM), jnp.float32)
    for l in range(DEPTH):
        last = l == DEPTH - 1
        mod = silu_c @ w_ada[l] + b_ada[l]
        shift, scale, gate = jnp.split(mod[:, None, :], 3, axis=-1)
        mod_c = silu_cc @ w_ada[l] + b_ada[l]
        shift_c, scale_c, gate_c = jnp.split(mod_c, 3, axis=-1)
        h = rmsnorm(x, norm_g[l]) * (1.0 + scale) + shift
        hc = rmsnorm(xc, norm_g[l]) * (1.0 + scale_c) + shift_c
        col_major = l % 2 == 1
        if col_major:
            h = to_col_major(h)
        p = h @ w_in[l]
        pc = hc @ (w_in[l][:, :A_COLS] if last else w_in[l])
        o_ctx, s_f, s_b = gdn_bidir(pc[..., :A_COLS], s_zero, s_zero, qkv_conv_w[l], a_log[l], dt_bias[l])
        o_lat, _, _ = gdn_bidir(p[..., :A_COLS], s_f, s_b, qkv_conv_w[l], a_log[l], dt_bias[l])
        y = mixer_output(o_lat, p, gdn_norm_g[l], short_conv_w[l], conf_conv_w[l], conf_conv_b[l],
                         conf_ln_g[l], conf_ln_b[l], smlp_ln_g[l], smlp_ln_b[l], smlp_w[l],
                         smlp_b[l], w_out[l])
        if col_major:
            y = from_col_major(y)
        x = x + gate * y
        if not last:
            yc = mixer_output(o_ctx, pc, gdn_norm_g[l], short_conv_w[l], conf_conv_w[l], conf_conv_b[l],
                              conf_ln_g[l], conf_ln_b[l], smlp_ln_g[l], smlp_ln_b[l], smlp_w[l],
                              smlp_b[l], w_out[l])
            xc = xc + gate_c * yc
    return rmsnorm(x, final_g)
```

```python
import functools

import jax
import jax.numpy as jnp
from jax import lax
from jax.experimental import pallas as pl
from jax.experimental.pallas import tpu as pltpu

F32 = jnp.float32
BF16 = jnp.bfloat16

D_MODEL = 1024
DEPTH = 4
GRID_W = 64
HEAD_DIM = 64
BRANCH_W = 256
N_HEADS = 4
A_COLS = 3 * BRANCH_W + 4 * N_HEADS
B_COLS = 3 * BRANCH_W
C_COLS = 2 * BRANCH_W
D_COLS = 2 * BRANCH_W
IN_COLS = A_COLS + B_COLS + C_COLS + D_COLS + D_MODEL
CONF_CONV = 31
GDN_CHUNK = 64
MLP_CHUNK = 128
EPS = 1e-6

LANES = 128
SUBLANES = 8
AB_PAD = LANES
MOD_ROWS = 24
TOKEN_TILE = 256
CONV_PAD = 16
VMEM_LIMIT = 56 * 1024 * 1024
NEG_BIG = -1e30


def _sigmoid(x):
    return 1.0 / (1.0 + jnp.exp(-x))


def _silu(x):
    return x * _sigmoid(x)


def _softplus(x):
    return jnp.maximum(x, 0.0) + jnp.log1p(jnp.exp(-jnp.abs(x)))


def _dot(a, b):
    return jnp.dot(a.astype(BF16), b.astype(BF16), preferred_element_type=F32)


def _dot_nt(a, b):
    return lax.dot_general(a.astype(BF16), b.astype(BF16), (((1,), (1,)), ((), ())),
                           preferred_element_type=F32)


def _dot_tn(a, b):
    return lax.dot_general(a.astype(BF16), b.astype(BF16), (((0,), (0,)), ((), ())),
                           preferred_element_type=F32)


def _split3(x):
    x1 = x.astype(BF16)
    r1 = x - x1.astype(F32)
    x2 = r1.astype(BF16)
    x3 = (r1 - x2.astype(F32)).astype(BF16)
    return x1, x2, x3


def _dot_exact_rhs01(x, m01):
    x1, x2, x3 = _split3(x)
    mb = m01.astype(BF16)
    acc = jnp.dot(x1, mb, preferred_element_type=F32)
    acc = acc + jnp.dot(x2, mb, preferred_element_type=F32)
    return acc + jnp.dot(x3, mb, preferred_element_type=F32)


def _dot_exact_lhs01(m01, x):
    x1, x2, x3 = _split3(x)
    mb = m01.astype(BF16)
    acc = jnp.dot(mb, x1, preferred_element_type=F32)
    acc = acc + jnp.dot(mb, x2, preferred_element_type=F32)
    return acc + jnp.dot(mb, x3, preferred_element_type=F32)


def _mod_kernel(c_ref, w_ref, b_ref, o_ref):
    s = _silu(c_ref[...])
    o_ref[...] = jnp.dot(s, w_ref[...], preferred_element_type=F32) + b_ref[...]


def _modulation(cvec, w_ada, b_ada):
    depth = w_ada.shape[0]
    return pl.pallas_call(
        _mod_kernel,
        grid=(depth, 3),
        in_specs=[
            pl.BlockSpec((MOD_ROWS, D_MODEL), lambda l, k: (0, 0)),
            pl.BlockSpec((None, D_MODEL, D_MODEL), lambda l, k: (l, 0, k)),
            pl.BlockSpec((None, 1, D_MODEL), lambda l, k: (l, 0, k)),
        ],
        out_specs=pl.BlockSpec((None, MOD_ROWS, D_MODEL), lambda l, k: (l, 0, k)),
        out_shape=jax.ShapeDtypeStruct((depth, MOD_ROWS, 3 * D_MODEL), F32),
        compiler_params=pltpu.CompilerParams(
            dimension_semantics=("arbitrary", "arbitrary"), vmem_limit_bytes=VMEM_LIMIT),
        name="modulation",
    )(cvec, w_ada, b_ada.reshape(depth, 1, 3 * D_MODEL))


def _inproj_kernel(x_ref, sh_ref, sc_ref, g_ref, w_ref, *refs, widths):
    out_refs = refs[:len(widths)]
    h_ref = refs[-1]
    rows = x_ref.shape[0]
    for j in range(x_ref.shape[1] // D_MODEL):
        x = x_ref[:, j * D_MODEL:(j + 1) * D_MODEL]
        ms = jnp.mean(x * x, axis=-1, keepdims=True)
        h = x * lax.rsqrt(ms + EPS) * g_ref[...]
        h = h * (1.0 + sc_ref[...]) + sh_ref[...]
        h_ref[j * rows:(j + 1) * rows, :] = h.astype(BF16)
    hb = h_ref[...]
    off = 0
    for o_ref, wd in zip(out_refs, widths):
        o_ref[...] = jnp.dot(hb, w_ref[:, off:off + wd], preferred_element_type=F32)
        off += wd


def _mod_spec(layer, kind, ctx):
    if ctx:
        return pl.BlockSpec((None, 1, D_MODEL), lambda b, t: (layer * MOD_ROWS + MOD_ROWS - 8, 0, kind))
    return pl.BlockSpec((None, 1, D_MODEL), lambda b, t: (layer * MOD_ROWS + b, 0, kind))


def _token_spec(bsz, seq, width, col_major):
    if not col_major:
        return (bsz, seq, width), pl.BlockSpec((None, TOKEN_TILE, width), lambda b, t: (b, t, 0))
    rows = seq // GRID_W
    return ((bsz, rows, GRID_W * width),
            pl.BlockSpec((None, rows, (TOKEN_TILE // rows) * width), lambda b, t: (b, 0, t)))


def _in_projection(x, mod3, norm_g_l, w_l, layer, widths, col_major, ctx):
    bsz, seq, _ = x.shape
    view, x_spec = _token_spec(bsz, seq, D_MODEL, col_major)
    n_cols = sum(widths)
    kern = functools.partial(_inproj_kernel, widths=tuple(widths))
    return pl.pallas_call(
        kern,
        grid=(bsz, seq // TOKEN_TILE),
        in_specs=[x_spec] + [
            _mod_spec(layer, 0, ctx), _mod_spec(layer, 1, ctx),
            pl.BlockSpec((1, D_MODEL), lambda b, t: (0, 0)),
            pl.BlockSpec((D_MODEL, n_cols), lambda b, t: (0, 0)),
        ],
        out_specs=[pl.BlockSpec((None, TOKEN_TILE, wd), lambda b, t: (b, t, 0)) for wd in widths],
        out_shape=[jax.ShapeDtypeStruct((bsz, seq, wd), F32) for wd in widths],
        scratch_shapes=[pltpu.VMEM((TOKEN_TILE, D_MODEL), BF16)],
        compiler_params=pltpu.CompilerParams(
            dimension_semantics=("arbitrary", "arbitrary"), vmem_limit_bytes=VMEM_LIMIT),
        name="in_projection",
    )(x.reshape(view), mod3, mod3, norm_g_l.reshape(1, D_MODEL), w_l)


def _outproj_kernel(ya_ref, yb_ref, pg_ref, w_ref, gate_ref, x_ref, fg_ref, o_ref, *, final):
    rows = x_ref.shape[0]
    sg = _silu(pg_ref[...])
    za = ya_ref[...] * sg[:, :BRANCH_W]
    zb = yb_ref[...] * sg[:, BRANCH_W:]
    r = _dot(za, w_ref[:BRANCH_W, :]) + _dot(zb, w_ref[BRANCH_W:, :])
    for j in range(x_ref.shape[1] // D_MODEL):
        cols = slice(j * D_MODEL, (j + 1) * D_MODEL)
        xn = x_ref[:, cols] + gate_ref[...] * r[j * rows:(j + 1) * rows, :]
        if final:
            ms = jnp.mean(xn * xn, axis=-1, keepdims=True)
            xn = xn * lax.rsqrt(ms + EPS) * fg_ref[...]
        o_ref[:, cols] = xn


def _out_projection(y_a, y_bcd, p_gate, w_out_l, mod3, x, layer, col_major, ctx, final_g, final):
    bsz, seq, _ = x.shape
    view, x_spec = _token_spec(bsz, seq, D_MODEL, col_major)
    tok = lambda wd: pl.BlockSpec((None, TOKEN_TILE, wd), lambda b, t: (b, t, 0))
    out = pl.pallas_call(
        functools.partial(_outproj_kernel, final=final),
        grid=(bsz, seq // TOKEN_TILE),
        in_specs=[tok(BRANCH_W), tok(3 * BRANCH_W), tok(D_MODEL),
                  pl.BlockSpec((D_MODEL, D_MODEL), lambda b, t: (0, 0)),
                  _mod_spec(layer, 2, ctx), x_spec,
                  pl.BlockSpec((1, D_MODEL), lambda b, t: (0, 0))],
        out_specs=x_spec,
        out_shape=jax.ShapeDtypeStruct(view, F32),
        compiler_params=pltpu.CompilerParams(
            dimension_semantics=("arbitrary", "arbitrary"), vmem_limit_bytes=VMEM_LIMIT),
        name="out_projection",
    )(y_a, y_bcd, p_gate, w_out_l, mod3, x.reshape(view), final_g.reshape(1, D_MODEL))
    return out.reshape(bsz, seq, D_MODEL)


def _gdn_kernel(pq_ref, pab_ref, cw_ref, alog_ref, dtb_ref, ng_ref, s0f_ref, s0b_ref,
                y_ref, sf_ref, sb_ref,
                qkv_ref, gx_ref, u_ref, w_ref, at_ref, qg_ref, kd_ref, gl_ref, o_ref):
    seq = pq_ref.shape[0]
    n_chunks = seq // GDN_CHUNK
    C = GDN_CHUNK
    W = BRANCH_W

    ri = lax.broadcasted_iota(jnp.int32, (C, W), 0)
    ci = lax.broadcasted_iota(jnp.int32, (C, W), 1)
    cj = ci & (C - 1)
    low_incl = ri >= cj
    low_strict = ri > cj
    up_incl = ri <= cj
    up_strict = ri < cj
    eye_cat = (ri == cj).astype(F32)
    r2 = lax.broadcasted_iota(jnp.int32, (W, W), 0)
    c2 = lax.broadcasted_iota(jnp.int32, (W, W), 1)
    blockmask = (r2 >> 6) == (c2 >> 6)
    t_r = lax.broadcasted_iota(jnp.int32, (C, C), 0)
    t_c = lax.broadcasted_iota(jnp.int32, (C, C), 1)
    tri_low = (t_r >= t_c).astype(F32)
    tri_up = (t_r <= t_c).astype(F32)
    e_r = lax.broadcasted_iota(jnp.int32, (AB_PAD, 4 * W), 0)
    e_c = lax.broadcasted_iota(jnp.int32, (AB_PAD, 4 * W), 1)
    expand = (e_r == ((e_c >> 8) * N_HEADS + ((e_c & (W - 1)) >> 6))).astype(F32)
    ones_blk = blockmask.astype(F32)
    lane = lax.broadcasted_iota(jnp.int32, (C, AB_PAD), 1)
    row64 = lax.broadcasted_iota(jnp.int32, (C, 3 * W), 0)

    def bd(x):
        return jnp.where(blockmask, jnp.concatenate([x] * N_HEADS, axis=0), 0.0)

    def cat_mm(a_cat, b_cat):
        return _dot(a_cat, bd(b_cat))

    def head_sum(x):
        return _dot_exact_rhs01(x, ones_blk)

    cw = cw_ref[...]

    def prep(c, carry):
        r0 = pl.multiple_of(c * C, C)
        cur = pq_ref[pl.ds(r0, C), :]
        prev8 = pq_ref[pl.ds(pl.multiple_of(jnp.maximum(r0 - SUBLANES, 0), SUBLANES), SUBLANES), :]
        next8 = pq_ref[pl.ds(pl.multiple_of(jnp.minimum(r0 + C, seq - SUBLANES), SUBLANES), SUBLANES), :]
        prev_row = prev8[SUBLANES - 1:SUBLANES, :] * (c > 0).astype(F32)
        next_row = next8[0:1, :] * (c < n_chunks - 1).astype(F32)
        up = jnp.where(row64 == 0, prev_row, pltpu.roll(cur, 1, 0))
        dn = jnp.where(row64 == C - 1, next_row, pltpu.roll(cur, C - 1, 0))
        conv = up * cw[0:1, :] + cur * cw[1:2, :] + dn * cw[2:3, :]
        act = _silu(conv)
        q = act[:, :W]
        k = act[:, W:2 * W]
        v = act[:, 2 * W:]
        q = q * lax.rsqrt(head_sum(q * q) + EPS) * (HEAD_DIM ** -0.5)
        k = k * lax.rsqrt(head_sum(k * k) + EPS)
        qkv_ref[pl.ds(r0, C), :W] = q
        qkv_ref[pl.ds(r0, C), W:2 * W] = k
        qkv_ref[pl.ds(r0, C), 2 * W:] = v
        ab = pab_ref[pl.ds(r0, C), :]
        gk = -jnp.exp(alog_ref[...]) * _softplus(ab + dtb_ref[...])
        gb = jnp.where(lane < 2 * N_HEADS, gk, _sigmoid(ab))
        gx_ref[pl.ds(r0, C), :] = _dot_exact_rhs01(gb, expand)
        return carry

    lax.fori_loop(0, n_chunks, prep, 0)

    def intra(c, carry):
        r0 = pl.multiple_of(c * C, C)
        q = qkv_ref[pl.ds(r0, C), :W]
        k = qkv_ref[pl.ds(r0, C), W:2 * W]
        v = qkv_ref[pl.ds(r0, C), 2 * W:]
        prod = _dot_nt(jnp.concatenate([q, k], axis=0), bd(k))
        qk = prod[:C]
        kk = prod[C:]
        for d in range(2):
            g = gx_ref[pl.ds(r0, C), d * W:(d + 1) * W]
            beta = gx_ref[pl.ds(r0, C), (2 + d) * W:(3 + d) * W]
            if d == 0:
                gi = _dot_exact_lhs01(tri_low, g)
                incl, strict = low_incl, low_strict
                g_end = gi[C - 1:C, :]
            else:
                gi = _dot_exact_lhs01(tri_up, g)
                incl, strict = up_incl, up_strict
                g_end = gi[0:1, :]
            gj = jnp.sum(gi * eye_cat, axis=0, keepdims=True)
            decay = jnp.exp(jnp.where(incl, gi - gj, NEG_BIG))
            eg = jnp.exp(gi)
            lmat = -jnp.where(strict, kk * beta * decay, 0.0)
            tinv = eye_cat + lmat
            power = lmat
            for _ in range(5):
                power = cat_mm(power, power)
                tinv = tinv + cat_mm(tinv, power)
            u_ref[d, pl.ds(r0, C), :] = cat_mm(tinv, v * beta)
            w_ref[d, pl.ds(r0, C), :] = cat_mm(tinv, k * beta * eg).astype(BF16)
            at_ref[d, pl.ds(r0, C), :] = jnp.where(incl, qk * decay, 0.0).astype(BF16)
            qg_ref[d, pl.ds(r0, C), :] = (q * eg).astype(BF16)
            kd_ref[d, pl.ds(r0, C), :] = (k * jnp.exp(g_end - gi)).astype(BF16)
            gl_ref[d, pl.ds(pl.multiple_of(c * SUBLANES, SUBLANES), SUBLANES), :] = jnp.broadcast_to(
                jnp.exp(g_end), (SUBLANES, W))
        return carry

    lax.fori_loop(0, n_chunks, intra, 0)

    def scan_step(d, c, s):
        r0 = pl.multiple_of(c * C, C)
        lhs = jnp.concatenate([qg_ref[d, pl.ds(r0, C), :], w_ref[d, pl.ds(r0, C), :]], axis=0)
        t = jnp.dot(lhs, s.astype(BF16), preferred_element_type=F32)
        v_new = u_ref[d, pl.ds(r0, C), :] - t[C:]
        o = t[:C] + jnp.dot(at_ref[d, pl.ds(r0, C), :], bd(v_new).astype(BF16), preferred_element_type=F32)
        gl = gl_ref[d, pl.ds(pl.multiple_of(c * SUBLANES, SUBLANES), 1), :]
        upd = _dot_tn(kd_ref[d, pl.ds(r0, C), :], v_new)
        s = s * gl + jnp.where(blockmask, upd, 0.0)
        return o, s

    def scan(i, carry):
        s_f, s_b = carry
        o_f, s_f = scan_step(0, i, s_f)
        o_ref[0, pl.ds(pl.multiple_of(i * C, C), C), :] = o_f
        cb = n_chunks - 1 - i
        o_b, s_b = scan_step(1, cb, s_b)
        o_ref[1, pl.ds(pl.multiple_of(cb * C, C), C), :] = o_b
        return s_f, s_b

    s_f, s_b = lax.fori_loop(0, n_chunks, scan, (s0f_ref[...], s0b_ref[...]))
    sf_ref[...] = s_f
    sb_ref[...] = s_b

    def finish(c, carry):
        r0 = pl.multiple_of(c * C, C)
        o = o_ref[0, pl.ds(r0, C), :] + o_ref[1, pl.ds(r0, C), :]
        ms = head_sum(o * o) * (1.0 / HEAD_DIM)
        y_ref[pl.ds(r0, C), :] = o * lax.rsqrt(ms + EPS) * ng_ref[...]
        return carry

    lax.fori_loop(0, n_chunks, finish, 0)


def _gdn_mixer(p_qkv, p_ab, conv_w, a_log_l, dt_bias_l, gdn_norm_g_l, s0_f, s0_b):
    bsz, seq, _ = p_qkv.shape
    W = BRANCH_W
    n_chunks = seq // GDN_CHUNK
    pad8 = jnp.zeros((AB_PAD - 2 * N_HEADS,), F32)
    alog_row = jnp.concatenate([a_log_l.reshape(-1), pad8]).reshape(1, AB_PAD)
    dtb_row = jnp.concatenate([dt_bias_l.reshape(-1), pad8]).reshape(1, AB_PAD)
    ng_row = jnp.tile(gdn_norm_g_l, N_HEADS).reshape(1, W)
    per_b = lambda r, wd: pl.BlockSpec((None, r, wd), lambda b: (b, 0, 0))
    const = lambda r, wd: pl.BlockSpec((r, wd), lambda b: (0, 0))
    return pl.pallas_call(
        _gdn_kernel,
        grid=(bsz,),
        in_specs=[per_b(seq, 3 * W), per_b(seq, AB_PAD), const(3, 3 * W), const(1, AB_PAD),
                  const(1, AB_PAD), const(1, W), per_b(W, W), per_b(W, W)],
        out_specs=[per_b(seq, W), per_b(W, W), per_b(W, W)],
        out_shape=[jax.ShapeDtypeStruct((bsz, seq, W), F32),
                   jax.ShapeDtypeStruct((bsz, W, W), F32),
                   jax.ShapeDtypeStruct((bsz, W, W), F32)],
        scratch_shapes=[
            pltpu.VMEM((seq, 3 * W), F32),
            pltpu.VMEM((seq, 4 * W), F32),
            pltpu.VMEM((2, seq, W), F32),
            pltpu.VMEM((2, seq, W), BF16),
            pltpu.VMEM((2, seq, W), BF16),
            pltpu.VMEM((2, seq, W), BF16),
            pltpu.VMEM((2, seq, W), BF16),
            pltpu.VMEM((2, n_chunks * SUBLANES, W), F32),
            pltpu.VMEM((2, seq, W), F32),
        ],
        compiler_params=pltpu.CompilerParams(
            dimension_semantics=("arbitrary",), vmem_limit_bytes=VMEM_LIMIT),
        name="gdn_mixer",
    )(p_qkv, p_ab, conv_w, alog_row, dtb_row, ng_row, s0_f, s0_b)


def _layernorm(x, g, b):
    mu = jnp.mean(x, axis=-1, keepdims=True)
    xc = x - mu
    var = jnp.mean(xc * xc, axis=-1, keepdims=True)
    return xc * lax.rsqrt(var + EPS) * g + b


def _bcd_kernel(pb_ref, pcd_ref, sw_ref, cw_ref, cb_ref, clg_ref, clb_ref, slg_ref, slb_ref,
                ws_ref, bs_ref, y_ref, z_ref, vln_ref):
    seq = pb_ref.shape[0]
    W = BRANCH_W
    T = GDN_CHUNK
    n_tiles = seq // T
    row = lax.broadcasted_iota(jnp.int32, (T, W), 0)
    sw = sw_ref[...]
    cw = cw_ref[...]

    z_ref[0:CONV_PAD, :] = jnp.zeros((CONV_PAD, W), F32)
    z_ref[CONV_PAD + seq:, :] = jnp.zeros((CONV_PAD, W), F32)

    def stage(t, carry):
        r0 = pl.multiple_of(t * T, T)
        cd = pcd_ref[pl.ds(r0, T), :]
        z_ref[pl.ds(pl.multiple_of(r0 + CONV_PAD, SUBLANES), T), :] = cd[:, :W] * _sigmoid(cd[:, W:2 * W])
        vln_ref[pl.ds(r0, T), :] = _layernorm(cd[:, 3 * W:], slg_ref[...], slb_ref[...])
        return carry

    lax.fori_loop(0, n_tiles, stage, 0)

    def convs(t, carry):
        r0 = pl.multiple_of(t * T, T)
        pb = pb_ref[pl.ds(r0, T), :]
        cur = pb[:, W:2 * W] * pb[:, 2 * W:]
        pv = pb_ref[pl.ds(pl.multiple_of(jnp.maximum(r0 - SUBLANES, 0), SUBLANES), SUBLANES), :]
        nx = pb_ref[pl.ds(pl.multiple_of(jnp.minimum(r0 + T, seq - SUBLANES), SUBLANES), SUBLANES), :]
        prev_row = (pv[:, W:2 * W] * pv[:, 2 * W:])[SUBLANES - 1:SUBLANES, :] * (t > 0).astype(F32)
        next_row = (nx[:, W:2 * W] * nx[:, 2 * W:])[0:1, :] * (t < n_tiles - 1).astype(F32)
        up = jnp.where(row == 0, prev_row, pltpu.roll(cur, 1, 0))
        dn = jnp.where(row == T - 1, next_row, pltpu.roll(cur, T - 1, 0))
        y_ref[pl.ds(r0, T), :W] = pb[:, :W] * (up * sw[0:1, :] + cur * sw[1:2, :] + dn * sw[2:3, :])
        acc = jnp.zeros((T, W), F32)
        for r in range(SUBLANES):
            part = None
            for a in range(4):
                o = SUBLANES * a + r
                if o < 1 or o > CONF_CONV:
                    continue
                win = z_ref[pl.ds(pl.multiple_of(r0 + SUBLANES * a, SUBLANES), T + SUBLANES), :]
                term = win * cw[o - 1:o, :]
                part = term if part is None else part + term
            if r:
                part = pltpu.roll(part, T + SUBLANES - r, 0)
            acc = acc + part[:T, :]
        zc = _layernorm(acc + cb_ref[...], clg_ref[...], clb_ref[...])
        y_ref[pl.ds(r0, T), W:2 * W] = _silu(zc)
        return carry

    lax.fori_loop(0, n_tiles, convs, 0)

    lane_grp = lax.broadcasted_iota(jnp.int32, (MLP_CHUNK, W), 1) >> 6
    wsb = ws_ref[...].astype(BF16)

    def smlp(n, carry):
        r0 = pl.multiple_of(n * MLP_CHUNK, MLP_CHUNK)
        full = jnp.dot(wsb, vln_ref[pl.ds(r0, MLP_CHUNK), :].astype(BF16), preferred_element_type=F32)
        mixed = bs_ref[...]
        for g in range(N_HEADS):
            mixed = mixed + jnp.where(lane_grp == g, full[g * MLP_CHUNK:(g + 1) * MLP_CHUNK, :], 0.0)
        y_ref[pl.ds(r0, MLP_CHUNK), 2 * W:] = pcd_ref[pl.ds(r0, MLP_CHUNK), 2 * W:3 * W] * mixed
        return carry

    lax.fori_loop(0, seq // MLP_CHUNK, smlp, 0)


def _bcd_mixer(p_b, p_cd, short_w, conf_w, conf_b, conf_g, conf_beta, smlp_g, smlp_beta, smlp_w, smlp_b):
    bsz, seq, _ = p_b.shape
    W = BRANCH_W
    ws_rows = smlp_w.reshape(N_HEADS * MLP_CHUNK, MLP_CHUNK)
    bs_exp = jnp.repeat(smlp_b.T, HEAD_DIM, axis=1)
    row = lambda a: a.reshape(1, W)
    per_b = lambda wd: pl.BlockSpec((None, seq, wd), lambda b: (b, 0, 0))
    const = lambda r, wd: pl.BlockSpec((r, wd), lambda b: (0, 0))
    return pl.pallas_call(
        _bcd_kernel,
        grid=(bsz,),
        in_specs=[per_b(3 * W), per_b(4 * W), const(3, W), const(CONF_CONV, W), const(1, W), const(1, W),
                  const(1, W), const(1, W), const(1, W), const(N_HEADS * MLP_CHUNK, MLP_CHUNK),
                  const(MLP_CHUNK, W)],
        out_specs=per_b(3 * W),
        out_shape=jax.ShapeDtypeStruct((bsz, seq, 3 * W), F32),
        scratch_shapes=[pltpu.VMEM((seq + 2 * CONV_PAD, W), F32), pltpu.VMEM((seq, W), F32)],
        compiler_params=pltpu.CompilerParams(
            dimension_semantics=("arbitrary",), vmem_limit_bytes=VMEM_LIMIT),
        name="bcd_mixer",
    )(p_b, p_cd, short_w, conf_w, row(conf_b), row(conf_g), row(conf_beta), row(smlp_g), row(smlp_beta),
      ws_rows, bs_exp)


def _permute_w_in(w_in):
    depth = w_in.shape[0]
    b0 = A_COLS
    c0 = b0 + B_COLS
    g0 = c0 + C_COLS + D_COLS
    pad = jnp.zeros((depth, D_MODEL, AB_PAD - 4 * N_HEADS), w_in.dtype)
    return jnp.concatenate(
        [w_in[:, :, :3 * BRANCH_W], w_in[:, :, b0:c0], w_in[:, :, c0:g0], w_in[:, :, g0:],
         w_in[:, :, 3 * BRANCH_W:A_COLS], pad], axis=-1).astype(BF16)


FULL_WIDTHS = (3 * BRANCH_W, 3 * BRANCH_W, 4 * BRANCH_W, D_MODEL, AB_PAD)


def kernel(x, c, ctx, c_ctx, norm_g, w_ada, b_ada, w_in, qkv_conv_w, a_log, dt_bias, gdn_norm_g,
           short_conv_w, conf_conv_w, conf_conv_b, conf_ln_g, conf_ln_b, smlp_ln_g, smlp_ln_b,
           smlp_w, smlp_b, w_out, final_g):
    bsz, seq, _ = x.shape
    depth = w_in.shape[0]
    assert bsz + 1 <= MOD_ROWS - 7 and seq % TOKEN_TILE == 0 and ctx.shape[1] % TOKEN_TILE == 0

    cvec = jnp.zeros((MOD_ROWS, D_MODEL), F32).at[:bsz].set(c).at[MOD_ROWS - 8].set(c_ctx)
    mod3 = _modulation(cvec, w_ada, b_ada).reshape(depth * MOD_ROWS, 1, 3 * D_MODEL)
    w_perm = _permute_w_in(w_in)
    w_out_b = w_out.astype(BF16)
    s_zero = jnp.zeros((bsz, BRANCH_W, BRANCH_W), F32)
    qkv_only = 3 * BRANCH_W

    xc = ctx
    for l in range(depth):
        last = l == depth - 1
        col_major = l % 2 == 1
        mix_args = (short_conv_w[l], conf_conv_w[l], conf_conv_b[l], conf_ln_g[l], conf_ln_b[l],
                    smlp_ln_g[l], smlp_ln_b[l], smlp_w[l], smlp_b[l])
        if last:
            w_ctx = jnp.concatenate([w_perm[l][:, :qkv_only], w_perm[l][:, -AB_PAD:]], axis=-1)
            pc_qkv, pc_ab = _in_projection(xc, mod3, norm_g[l], w_ctx, l, (qkv_only, AB_PAD), False, True)
        else:
            pc_qkv, pc_b, pc_cd, pc_gate, pc_ab = _in_projection(
                xc, mod3, norm_g[l], w_perm[l], l, FULL_WIDTHS, False, True)
        yc_a, s_f, s_b = _gdn_mixer(pc_qkv, pc_ab, qkv_conv_w[l], a_log[l], dt_bias[l], gdn_norm_g[l],
                                    s_zero, s_zero)
        p_qkv, p_b, p_cd, p_gate, p_ab = _in_projection(
            x, mod3, norm_g[l], w_perm[l], l, FULL_WIDTHS, col_major, False)
        y_a, _, _ = _gdn_mixer(p_qkv, p_ab, qkv_conv_w[l], a_log[l], dt_bias[l], gdn_norm_g[l], s_f, s_b)
        y_bcd = _bcd_mixer(p_b, p_cd, *mix_args)
        x = _out_projection(y_a, y_bcd, p_gate, w_out_b[l], mod3, x, l, col_major, False, final_g, last)
        if not last:
            yc_bcd = _bcd_mixer(pc_b, pc_cd, *mix_args)
            xc = _out_projection(yc_a, yc_bcd, pc_gate, w_out_b[l], mod3, xc, l, False, True, final_g, False)
    return x
```

```python
import functools

import jax
import jax.numpy as jnp
from jax import lax
from jax.experimental import pallas as pl
from jax.experimental.pallas import tpu as pltpu

F32 = jnp.float32
BF16 = jnp.bfloat16

D_MODEL = 1024
DEPTH = 4
GRID_W = 64
HEAD_DIM = 64
BRANCH_W = 256
N_HEADS = 4
A_COLS = 3 * BRANCH_W + 4 * N_HEADS
B_COLS = 3 * BRANCH_W
C_COLS = 2 * BRANCH_W
D_COLS = 2 * BRANCH_W
IN_COLS = A_COLS + B_COLS + C_COLS + D_COLS + D_MODEL
CONF_CONV = 31
GDN_CHUNK = 64
MLP_CHUNK = 128
EPS = 1e-6

LANES = 128
SUBLANES = 8
AB_PAD = LANES
MOD_ROWS = 24
TOKEN_TILE = 256
CONV_PAD = 16
VMEM_LIMIT = 56 * 1024 * 1024
NEG_BIG = -1e30
GDN_INTERLEAVE = 4


def _sigmoid(x):
    return 1.0 / (1.0 + jnp.exp(-x))


def _silu(x):
    return x * _sigmoid(x)


def _softplus(x):
    return jnp.maximum(x, 0.0) + jnp.log1p(jnp.exp(-jnp.abs(x)))


def _dot(a, b):
    return jnp.dot(a.astype(BF16), b.astype(BF16), preferred_element_type=F32)


def _dot_nt(a, b):
    return lax.dot_general(a.astype(BF16), b.astype(BF16), (((1,), (1,)), ((), ())),
                           preferred_element_type=F32)


def _dot_tn(a, b):
    return lax.dot_general(a.astype(BF16), b.astype(BF16), (((0,), (0,)), ((), ())),
                           preferred_element_type=F32)


def _split3(x):
    x1 = x.astype(BF16)
    r1 = x - x1.astype(F32)
    x2 = r1.astype(BF16)
    x3 = (r1 - x2.astype(F32)).astype(BF16)
    return x1, x2, x3


def _mod_kernel(c_ref, w_ref, b_ref, o_ref):
    s = _silu(c_ref[...])
    o_ref[...] = jnp.dot(s, w_ref[...], preferred_element_type=F32) + b_ref[...]


def _modulation(cvec, w_ada, b_ada):
    depth = w_ada.shape[0]
    return pl.pallas_call(
        _mod_kernel,
        grid=(depth, 3),
        in_specs=[
            pl.BlockSpec((MOD_ROWS, D_MODEL), lambda l, k: (0, 0)),
            pl.BlockSpec((None, D_MODEL, D_MODEL), lambda l, k: (l, 0, k)),
            pl.BlockSpec((None, 1, D_MODEL), lambda l, k: (l, 0, k)),
        ],
        out_specs=pl.BlockSpec((None, MOD_ROWS, D_MODEL), lambda l, k: (l, 0, k)),
        out_shape=jax.ShapeDtypeStruct((depth, MOD_ROWS, 3 * D_MODEL), F32),
        compiler_params=pltpu.CompilerParams(
            dimension_semantics=("arbitrary", "arbitrary"), vmem_limit_bytes=VMEM_LIMIT),
        name="modulation",
    )(cvec, w_ada, b_ada.reshape(depth, 1, 3 * D_MODEL))


def _inproj_kernel(x_ref, sh_ref, sc_ref, g_ref, w_ref, *refs, widths):
    out_refs = refs[:len(widths)]
    h_ref = refs[-1]
    rows = x_ref.shape[0]
    for j in range(x_ref.shape[1] // D_MODEL):
        x = x_ref[:, j * D_MODEL:(j + 1) * D_MODEL]
        ms = jnp.mean(x * x, axis=-1, keepdims=True)
        h = x * lax.rsqrt(ms + EPS) * g_ref[...]
        h = h * (1.0 + sc_ref[...]) + sh_ref[...]
        h_ref[j * rows:(j + 1) * rows, :] = h.astype(BF16)
    hb = h_ref[...]
    off = 0
    for o_ref, wd in zip(out_refs, widths):
        o_ref[...] = jnp.dot(hb, w_ref[:, off:off + wd], preferred_element_type=F32)
        off += wd


def _mod_spec(layer, kind, ctx):
    if ctx:
        return pl.BlockSpec((None, 1, D_MODEL), lambda b, t: (layer * MOD_ROWS + MOD_ROWS - 8, 0, kind))
    return pl.BlockSpec((None, 1, D_MODEL), lambda b, t: (layer * MOD_ROWS + b, 0, kind))


def _token_spec(bsz, seq, width, col_major):
    if not col_major:
        return (bsz, seq, width), pl.BlockSpec((None, TOKEN_TILE, width), lambda b, t: (b, t, 0))
    rows = seq // GRID_W
    return ((bsz, rows, GRID_W * width),
            pl.BlockSpec((None, rows, (TOKEN_TILE // rows) * width), lambda b, t: (b, 0, t)))


def _in_projection(x, mod3, norm_g_l, w_l, layer, widths, col_major, ctx):
    bsz, seq, _ = x.shape
    view, x_spec = _token_spec(bsz, seq, D_MODEL, col_major)
    n_cols = sum(widths)
    kern = functools.partial(_inproj_kernel, widths=tuple(widths))
    return pl.pallas_call(
        kern,
        grid=(bsz, seq // TOKEN_TILE),
        in_specs=[x_spec] + [
            _mod_spec(layer, 0, ctx), _mod_spec(layer, 1, ctx),
            pl.BlockSpec((1, D_MODEL), lambda b, t: (0, 0)),
            pl.BlockSpec((D_MODEL, n_cols), lambda b, t: (0, 0)),
        ],
        out_specs=[pl.BlockSpec((None, TOKEN_TILE, wd), lambda b, t: (b, t, 0)) for wd in widths],
        out_shape=[jax.ShapeDtypeStruct((bsz, seq, wd), F32) for wd in widths],
        scratch_shapes=[pltpu.VMEM((TOKEN_TILE, D_MODEL), BF16)],
        compiler_params=pltpu.CompilerParams(
            dimension_semantics=("arbitrary", "arbitrary"), vmem_limit_bytes=VMEM_LIMIT),
        name="in_projection",
    )(x.reshape(view), mod3, mod3, norm_g_l.reshape(1, D_MODEL), w_l)


def _outproj_kernel(ya_ref, yb_ref, pg_ref, w_ref, gate_ref, x_ref, fg_ref, o_ref, *, final):
    rows = x_ref.shape[0]
    sg = _silu(pg_ref[...])
    za = ya_ref[...] * sg[:, :BRANCH_W]
    zb = yb_ref[...] * sg[:, BRANCH_W:]
    r = _dot(za, w_ref[:BRANCH_W, :]) + _dot(zb, w_ref[BRANCH_W:, :])
    for j in range(x_ref.shape[1] // D_MODEL):
        cols = slice(j * D_MODEL, (j + 1) * D_MODEL)
        xn = x_ref[:, cols] + gate_ref[...] * r[j * rows:(j + 1) * rows, :]
        if final:
            ms = jnp.mean(xn * xn, axis=-1, keepdims=True)
            xn = xn * lax.rsqrt(ms + EPS) * fg_ref[...]
        o_ref[:, cols] = xn


def _out_projection(y_a, y_bcd, p_gate, w_out_l, mod3, x, layer, col_major, ctx, final_g, final):
    bsz, seq, _ = x.shape
    view, x_spec = _token_spec(bsz, seq, D_MODEL, col_major)
    tok = lambda wd: pl.BlockSpec((None, TOKEN_TILE, wd), lambda b, t: (b, t, 0))
    out = pl.pallas_call(
        functools.partial(_outproj_kernel, final=final),
        grid=(bsz, seq // TOKEN_TILE),
        in_specs=[tok(BRANCH_W), tok(3 * BRANCH_W), tok(D_MODEL),
                  pl.BlockSpec((D_MODEL, D_MODEL), lambda b, t: (0, 0)),
                  _mod_spec(layer, 2, ctx), x_spec,
                  pl.BlockSpec((1, D_MODEL), lambda b, t: (0, 0))],
        out_specs=x_spec,
        out_shape=jax.ShapeDtypeStruct(view, F32),
        compiler_params=pltpu.CompilerParams(
            dimension_semantics=("arbitrary", "arbitrary"), vmem_limit_bytes=VMEM_LIMIT),
        name="out_projection",
    )(y_a, y_bcd, p_gate, w_out_l, mod3, x.reshape(view), final_g.reshape(1, D_MODEL))
    return out.reshape(bsz, seq, D_MODEL)


def _gdn_kernel(pq_ref, pab_ref, cw_ref, alog_ref, dtb_ref, ng_ref, s0f_ref, s0b_ref,
                y_ref, sf_ref, sb_ref,
                qp_ref, op_ref, mp_ref, np_ref, gl_ref, o_ref):
    seq = pq_ref.shape[0]
    n_chunks = seq // GDN_CHUNK
    C = GDN_CHUNK
    W = BRANCH_W
    NG = 4 * N_HEADS

    ri = lax.broadcasted_iota(jnp.int32, (C, W), 0)
    ci = lax.broadcasted_iota(jnp.int32, (C, W), 1)
    cj = ci & (C - 1)
    low_incl = ri >= cj
    low_strict = ri > cj
    up_incl = ri <= cj
    up_strict = ri < cj
    eye_cat = (ri == cj).astype(F32)
    r2 = lax.broadcasted_iota(jnp.int32, (W, W), 0)
    c2 = lax.broadcasted_iota(jnp.int32, (W, W), 1)
    blockmask = (r2 >> 6) == (c2 >> 6)
    ones_blk = blockmask.astype(BF16)
    e_r = lax.broadcasted_iota(jnp.int32, (NG, 4 * W), 0)
    e_c = lax.broadcasted_iota(jnp.int32, (NG, 4 * W), 1)
    expand = (e_r == ((e_c >> 8) * N_HEADS + ((e_c & (W - 1)) >> 6))).astype(BF16)
    lane = lax.broadcasted_iota(jnp.int32, (C, AB_PAD), 1)
    rown = lax.broadcasted_iota(jnp.int32, (C, AB_PAD), 0)
    row64 = lax.broadcasted_iota(jnp.int32, (C, 3 * W), 0)

    def tile_heads(x):
        return jnp.concatenate([x] * N_HEADS, axis=0)

    def bd(x):
        return jnp.where(blockmask, tile_heads(x), 0.0)

    def fold(m):
        m = jnp.where(blockmask, m, 0.0)
        return m[:C] + m[C:2 * C] + m[2 * C:3 * C] + m[3 * C:]

    def cat_mm(a_cat, b_cat):
        return _dot(a_cat, bd(b_cat))

    def head_sums(xs):
        hi = [x.astype(BF16) for x in xs]
        lo = [(x - h.astype(F32)).astype(BF16) for x, h in zip(xs, hi)]
        r = jnp.dot(jnp.concatenate(hi + lo, axis=0), ones_blk, preferred_element_type=F32)
        n, m = len(xs), xs[0].shape[0]
        return [r[i * m:(i + 1) * m] + r[(n + i) * m:(n + i + 1) * m] for i in range(n)]

    def chunk_cumsum(x, reverse):
        s = 1
        while s < C:
            if reverse:
                x = x + jnp.where(rown < C - s, pltpu.roll(x, C - s, 0), 0.0)
            else:
                x = x + jnp.where(rown >= s, pltpu.roll(x, s, 0), 0.0)
            s *= 2
        return x

    cw = cw_ref[...]

    def conv_act(c):
        r0 = pl.multiple_of(c * C, C)
        cur = pq_ref[pl.ds(r0, C), :]
        prev8 = pq_ref[pl.ds(pl.multiple_of(jnp.maximum(r0 - SUBLANES, 0), SUBLANES), SUBLANES), :]
        next8 = pq_ref[pl.ds(pl.multiple_of(jnp.minimum(r0 + C, seq - SUBLANES), SUBLANES), SUBLANES), :]
        prev_row = prev8[SUBLANES - 1:SUBLANES, :] * (c > 0).astype(F32)
        next_row = next8[0:1, :] * (c < n_chunks - 1).astype(F32)
        up = jnp.where(row64 == 0, prev_row, pltpu.roll(cur, 1, 0))
        dn = jnp.where(row64 == C - 1, next_row, pltpu.roll(cur, C - 1, 0))
        return _silu(up * cw[0:1, :] + cur * cw[1:2, :] + dn * cw[2:3, :])

    def gate_sums(c):
        ab = pab_ref[pl.ds(pl.multiple_of(c * C, C), C), :]
        gk = -jnp.exp(alog_ref[...]) * _softplus(ab + dtb_ref[...])
        gb = jnp.where(lane < 2 * N_HEADS, gk, _sigmoid(ab))
        nar = jnp.where(lane < N_HEADS, chunk_cumsum(gb, False),
                        jnp.where(lane < 2 * N_HEADS, chunk_cumsum(gb, True), gb))
        return _split3(nar[:, :NG])

    def parallel_pass(i, carry):
        G = GDN_INTERLEAVE
        cs = [i * G + j for j in range(G)]
        chains = [(j, d) for j in range(G) for d in range(2)]
        acts = [conv_act(c) for c in cs]
        nars = [gate_sums(c) for c in cs]
        sums = [head_sums([a[:, :W] * a[:, :W], a[:, W:2 * W] * a[:, W:2 * W]]) for a in acts]
        q = [a[:, :W] * lax.rsqrt(s[0] + EPS) * (HEAD_DIM ** -0.5) for a, s in zip(acts, sums)]
        k = [a[:, W:2 * W] * lax.rsqrt(s[1] + EPS) for a, s in zip(acts, sums)]
        v = [a[:, 2 * W:] for a in acts]
        gx = [sum(jnp.dot(n, expand, preferred_element_type=F32) for n in n3) for n3 in nars]
        prod = [_dot_nt(jnp.concatenate([q[j], k[j]], axis=0), bd(k[j])) for j in range(G)]
        gi = [gx[j][:, d * W:(d + 1) * W] for j, d in chains]
        beta = [gx[j][:, (2 + d) * W:(3 + d) * W] for j, d in chains]
        incl = [up_incl if d else low_incl for _, d in chains]
        strict = [up_strict if d else low_strict for _, d in chains]
        g_end = [g[0:1, :] if d else g[C - 1:C, :] for g, (_, d) in zip(gi, chains)]
        gj = [jnp.sum(g * eye_cat, axis=0, keepdims=True) for g in gi]
        decay = [jnp.exp(jnp.where(m, g - g2, NEG_BIG)) for m, g, g2 in zip(incl, gi, gj)]
        eg = [jnp.exp(g) for g in gi]
        lmat = [-jnp.where(m, prod[j][C:] * b * dc, 0.0)
                for m, (j, _), b, dc in zip(strict, chains, beta, decay)]
        tinv = [eye_cat + l for l in lmat]
        power = [cat_mm(l, l) for l in lmat]
        for _ in range(4):
            st = [cat_mm(jnp.concatenate([t, p], axis=0), p) for t, p in zip(tinv, power)]
            tinv = [t + s[:C] for t, s in zip(tinv, st)]
            power = [s[C:] for s in st]
        tinv = [(t + cat_mm(t, p)).astype(BF16) for t, p in zip(tinv, power)]
        u = [jnp.dot(t, bd(v[j] * b).astype(BF16), preferred_element_type=F32)
             for t, (j, _), b in zip(tinv, chains, beta)]
        w = [jnp.dot(t, bd(k[j] * b * e).astype(BF16), preferred_element_type=F32)
             for t, (j, _), b, e in zip(tinv, chains, beta, eg)]
        attn = [jnp.where(m, prod[j][:C] * dc, 0.0).astype(BF16) for m, (j, _), dc in zip(incl, chains, decay)]
        au = [jnp.dot(a, bd(x).astype(BF16), preferred_element_type=F32) for a, x in zip(attn, u)]
        aw = [jnp.dot(a, bd(x).astype(BF16), preferred_element_type=F32) for a, x in zip(attn, w)]
        kdec = [k[j] * jnp.exp(ge - g) for (j, _), ge, g in zip(chains, g_end, gi)]
        kw = [_dot_tn(kd, x) for kd, x in zip(kdec, w)]
        ku = [_dot_tn(kd, x) for kd, x in zip(kdec, u)]
        for n, (j, d) in enumerate(chains):
            rows = pl.ds(pl.multiple_of(cs[j] * C, C), C)
            qp_ref[d, rows, :] = (q[j] * eg[n] - aw[n]).astype(BF16)
            op_ref[d, rows, :] = au[n]
            mp_ref[d, rows, :] = (-fold(kw[n])).astype(BF16)
            np_ref[d, rows, :] = fold(ku[n])
            gl_ref[d, pl.ds(pl.multiple_of(cs[j] * SUBLANES, SUBLANES), SUBLANES), :] = jnp.broadcast_to(
                jnp.exp(g_end[n]), (SUBLANES, W))
        return carry

    lax.fori_loop(0, n_chunks // GDN_INTERLEAVE, parallel_pass, 0)

    maskb = blockmask.astype(BF16)

    def scan_step(d, c, s):
        rows = pl.ds(pl.multiple_of(c * C, C), C)
        lhs = jnp.concatenate([qp_ref[d, rows, :], tile_heads(mp_ref[d, rows, :]) * maskb], axis=0)
        t = jnp.dot(lhs, s.astype(BF16), preferred_element_type=F32)
        o = t[:C] + op_ref[d, rows, :]
        gl = gl_ref[d, pl.ds(pl.multiple_of(c * SUBLANES, SUBLANES), 1), :]
        s = s * gl + t[C:] + bd(np_ref[d, rows, :])
        return o, s

    def scan(i, carry):
        s_f, s_b = carry
        o_f, s_f = scan_step(0, i, s_f)
        o_ref[0, pl.ds(pl.multiple_of(i * C, C), C), :] = o_f
        cb = n_chunks - 1 - i
        o_b, s_b = scan_step(1, cb, s_b)
        o_ref[1, pl.ds(pl.multiple_of(cb * C, C), C), :] = o_b
        return s_f, s_b

    s_f, s_b = lax.fori_loop(0, n_chunks, scan, (s0f_ref[...], s0b_ref[...]))
    sf_ref[...] = s_f
    sb_ref[...] = s_b

    def finish(c, carry):
        rows = pl.ds(pl.multiple_of(c * W, W), W)
        o = o_ref[0, rows, :] + o_ref[1, rows, :]
        ms = head_sums([o * o])[0] * (1.0 / HEAD_DIM)
        y_ref[rows, :] = o * lax.rsqrt(ms + EPS) * ng_ref[...]
        return carry

    lax.fori_loop(0, seq // W, finish, 0)


def _gdn_mixer(p_qkv, p_ab, conv_w, a_log_l, dt_bias_l, gdn_norm_g_l, s0_f, s0_b):
    bsz, seq, _ = p_qkv.shape
    W = BRANCH_W
    n_chunks = seq // GDN_CHUNK
    pad8 = jnp.zeros((AB_PAD - 2 * N_HEADS,), F32)
    alog_row = jnp.concatenate([a_log_l.reshape(-1), pad8]).reshape(1, AB_PAD)
    dtb_row = jnp.concatenate([dt_bias_l.reshape(-1), pad8]).reshape(1, AB_PAD)
    ng_row = jnp.tile(gdn_norm_g_l, N_HEADS).reshape(1, W)
    per_b = lambda r, wd: pl.BlockSpec((None, r, wd), lambda b: (b, 0, 0))
    const = lambda r, wd: pl.BlockSpec((r, wd), lambda b: (0, 0))
    return pl.pallas_call(
        _gdn_kernel,
        grid=(bsz,),
        in_specs=[per_b(seq, 3 * W), per_b(seq, AB_PAD), const(3, 3 * W), const(1, AB_PAD),
                  const(1, AB_PAD), const(1, W), per_b(W, W), per_b(W, W)],
        out_specs=[per_b(seq, W), per_b(W, W), per_b(W, W)],
        out_shape=[jax.ShapeDtypeStruct((bsz, seq, W), F32),
                   jax.ShapeDtypeStruct((bsz, W, W), F32),
                   jax.ShapeDtypeStruct((bsz, W, W), F32)],
        scratch_shapes=[
            pltpu.VMEM((2, seq, W), BF16),
            pltpu.VMEM((2, seq, W), F32),
            pltpu.VMEM((2, seq, W), BF16),
            pltpu.VMEM((2, seq, W), F32),
            pltpu.VMEM((2, n_chunks * SUBLANES, W), F32),
            pltpu.VMEM((2, seq, W), F32),
        ],
        compiler_params=pltpu.CompilerParams(
            dimension_semantics=("arbitrary",), vmem_limit_bytes=VMEM_LIMIT),
        name="gdn_mixer",
    )(p_qkv, p_ab, conv_w, alog_row, dtb_row, ng_row, s0_f, s0_b)


def _layernorm(x, g, b):
    mu = jnp.mean(x, axis=-1, keepdims=True)
    xc = x - mu
    var = jnp.mean(xc * xc, axis=-1, keepdims=True)
    return xc * lax.rsqrt(var + EPS) * g + b


def _bcd_kernel(pb_ref, pcd_ref, sw_ref, cw_ref, cb_ref, clg_ref, clb_ref, slg_ref, slb_ref,
                ws_ref, bs_ref, y_ref, z_ref, vln_ref):
    seq = pb_ref.shape[0]
    W = BRANCH_W
    T = GDN_CHUNK
    n_tiles = seq // T
    row = lax.broadcasted_iota(jnp.int32, (T, W), 0)
    sw = sw_ref[...]
    cw = cw_ref[...]

    z_ref[0:CONV_PAD, :] = jnp.zeros((CONV_PAD, W), F32)
    z_ref[CONV_PAD + seq:, :] = jnp.zeros((CONV_PAD, W), F32)

    def stage(t, carry):
        r0 = pl.multiple_of(t * T, T)
        cd = pcd_ref[pl.ds(r0, T), :]
        z_ref[pl.ds(pl.multiple_of(r0 + CONV_PAD, SUBLANES), T), :] = cd[:, :W] * _sigmoid(cd[:, W:2 * W])
        vln_ref[pl.ds(r0, T), :] = _layernorm(cd[:, 3 * W:], slg_ref[...], slb_ref[...])
        return carry

    lax.fori_loop(0, n_tiles, stage, 0)

    def convs(t, carry):
        r0 = pl.multiple_of(t * T, T)
        pb = pb_ref[pl.ds(r0, T), :]
        cur = pb[:, W:2 * W] * pb[:, 2 * W:]
        pv = pb_ref[pl.ds(pl.multiple_of(jnp.maximum(r0 - SUBLANES, 0), SUBLANES), SUBLANES), :]
        nx = pb_ref[pl.ds(pl.multiple_of(jnp.minimum(r0 + T, seq - SUBLANES), SUBLANES), SUBLANES), :]
        prev_row = (pv[:, W:2 * W] * pv[:, 2 * W:])[SUBLANES - 1:SUBLANES, :] * (t > 0).astype(F32)
        next_row = (nx[:, W:2 * W] * nx[:, 2 * W:])[0:1, :] * (t < n_tiles - 1).astype(F32)
        up = jnp.where(row == 0, prev_row, pltpu.roll(cur, 1, 0))
        dn = jnp.where(row == T - 1, next_row, pltpu.roll(cur, T - 1, 0))
        y_ref[pl.ds(r0, T), :W] = pb[:, :W] * (up * sw[0:1, :] + cur * sw[1:2, :] + dn * sw[2:3, :])
        acc = jnp.zeros((T, W), F32)
        for r in range(SUBLANES):
            part = None
            for a in range(4):
                o = SUBLANES * a + r
                if o < 1 or o > CONF_CONV:
                    continue
                win = z_ref[pl.ds(pl.multiple_of(r0 + SUBLANES * a, SUBLANES), T + SUBLANES), :]
                term = win * cw[o - 1:o, :]
                part = term if part is None else part + term
            if r:
                part = pltpu.roll(part, T + SUBLANES - r, 0)
            acc = acc + part[:T, :]
        zc = _layernorm(acc + cb_ref[...], clg_ref[...], clb_ref[...])
        y_ref[pl.ds(r0, T), W:2 * W] = _silu(zc)
        return carry

    lax.fori_loop(0, n_tiles, convs, 0)

    lane_grp = lax.broadcasted_iota(jnp.int32, (MLP_CHUNK, W), 1) >> 6
    wsb = ws_ref[...].astype(BF16)

    def smlp(n, carry):
        r0 = pl.multiple_of(n * MLP_CHUNK, MLP_CHUNK)
        full = jnp.dot(wsb, vln_ref[pl.ds(r0, MLP_CHUNK), :].astype(BF16), preferred_element_type=F32)
        mixed = bs_ref[...]
        for g in range(N_HEADS):
            mixed = mixed + jnp.where(lane_grp == g, full[g * MLP_CHUNK:(g + 1) * MLP_CHUNK, :], 0.0)
        y_ref[pl.ds(r0, MLP_CHUNK), 2 * W:] = pcd_ref[pl.ds(r0, MLP_CHUNK), 2 * W:3 * W] * mixed
        return carry

    lax.fori_loop(0, seq // MLP_CHUNK, smlp, 0)


def _bcd_mixer(p_b, p_cd, short_w, conf_w, conf_b, conf_g, conf_beta, smlp_g, smlp_beta, smlp_w, smlp_b):
    bsz, seq, _ = p_b.shape
    W = BRANCH_W
    ws_rows = smlp_w.reshape(N_HEADS * MLP_CHUNK, MLP_CHUNK)
    bs_exp = jnp.repeat(smlp_b.T, HEAD_DIM, axis=1)
    row = lambda a: a.reshape(1, W)
    per_b = lambda wd: pl.BlockSpec((None, seq, wd), lambda b: (b, 0, 0))
    const = lambda r, wd: pl.BlockSpec((r, wd), lambda b: (0, 0))
    return pl.pallas_call(
        _bcd_kernel,
        grid=(bsz,),
        in_specs=[per_b(3 * W), per_b(4 * W), const(3, W), const(CONF_CONV, W), const(1, W), const(1, W),
                  const(1, W), const(1, W), const(1, W), const(N_HEADS * MLP_CHUNK, MLP_CHUNK),
                  const(MLP_CHUNK, W)],
        out_specs=per_b(3 * W),
        out_shape=jax.ShapeDtypeStruct((bsz, seq, 3 * W), F32),
        scratch_shapes=[pltpu.VMEM((seq + 2 * CONV_PAD, W), F32), pltpu.VMEM((seq, W), F32)],
        compiler_params=pltpu.CompilerParams(
            dimension_semantics=("arbitrary",), vmem_limit_bytes=VMEM_LIMIT),
        name="bcd_mixer",
    )(p_b, p_cd, short_w, conf_w, row(conf_b), row(conf_g), row(conf_beta), row(smlp_g), row(smlp_beta),
      ws_rows, bs_exp)


def _permute_w_in(w_in):
    depth = w_in.shape[0]
    b0 = A_COLS
    c0 = b0 + B_COLS
    g0 = c0 + C_COLS + D_COLS
    pad = jnp.zeros((depth, D_MODEL, AB_PAD - 4 * N_HEADS), w_in.dtype)
    return jnp.concatenate(
        [w_in[:, :, :3 * BRANCH_W], w_in[:, :, b0:c0], w_in[:, :, c0:g0], w_in[:, :, g0:],
         w_in[:, :, 3 * BRANCH_W:A_COLS], pad], axis=-1).astype(BF16)


FULL_WIDTHS = (3 * BRANCH_W, 3 * BRANCH_W, 4 * BRANCH_W, D_MODEL, AB_PAD)


def kernel(x, c, ctx, c_ctx, norm_g, w_ada, b_ada, w_in, qkv_conv_w, a_log, dt_bias, gdn_norm_g,
           short_conv_w, conf_conv_w, conf_conv_b, conf_ln_g, conf_ln_b, smlp_ln_g, smlp_ln_b,
           smlp_w, smlp_b, w_out, final_g):
    bsz, seq, _ = x.shape
    depth = w_in.shape[0]
    assert bsz + 1 <= MOD_ROWS - 7 and seq % TOKEN_TILE == 0 and ctx.shape[1] % TOKEN_TILE == 0

    cvec = jnp.zeros((MOD_ROWS, D_MODEL), F32).at[:bsz].set(c).at[MOD_ROWS - 8].set(c_ctx)
    mod3 = _modulation(cvec, w_ada, b_ada).reshape(depth * MOD_ROWS, 1, 3 * D_MODEL)
    w_perm = _permute_w_in(w_in)
    w_out_b = w_out.astype(BF16)
    s_zero = jnp.zeros((bsz, BRANCH_W, BRANCH_W), F32)
    qkv_only = 3 * BRANCH_W

    xc = ctx
    for l in range(depth):
        last = l == depth - 1
        col_major = l % 2 == 1
        mix_args = (short_conv_w[l], conf_conv_w[l], conf_conv_b[l], conf_ln_g[l], conf_ln_b[l],
                    smlp_ln_g[l], smlp_ln_b[l], smlp_w[l], smlp_b[l])
        if last:
            w_ctx = jnp.concatenate([w_perm[l][:, :qkv_only], w_perm[l][:, -AB_PAD:]], axis=-1)
            pc_qkv, pc_ab = _in_projection(xc, mod3, norm_g[l], w_ctx, l, (qkv_only, AB_PAD), False, True)
        else:
            pc_qkv, pc_b, pc_cd, pc_gate, pc_ab = _in_projection(
                xc, mod3, norm_g[l], w_perm[l], l, FULL_WIDTHS, False, True)
        yc_a, s_f, s_b = _gdn_mixer(pc_qkv, pc_ab, qkv_conv_w[l], a_log[l], dt_bias[l], gdn_norm_g[l],
                                    s_zero, s_zero)
        p_qkv, p_b, p_cd, p_gate, p_ab = _in_projection(
            x, mod3, norm_g[l], w_perm[l], l, FULL_WIDTHS, col_major, False)
        y_a, _, _ = _gdn_mixer(p_qkv, p_ab, qkv_conv_w[l], a_log[l], dt_bias[l], gdn_norm_g[l], s_f, s_b)
        y_bcd = _bcd_mixer(p_b, p_cd, *mix_args)
        x = _out_projection(y_a, y_bcd, p_gate, w_out_b[l], mod3, x, l, col_major, False, final_g, last)
        if not last:
            yc_bcd = _bcd_mixer(pc_b, pc_cd, *mix_args)
            xc = _out_projection(yc_a, yc_bcd, pc_gate, w_out_b[l], mod3, xc, l, False, True, final_g, False)
    return x
```

```python
import functools

import jax
import jax.numpy as jnp
from jax import lax
from jax.experimental import pallas as pl
from jax.experimental.pallas import tpu as pltpu

F32 = jnp.float32
BF16 = jnp.bfloat16

D_MODEL = 1024
DEPTH = 4
GRID_W = 64
HEAD_DIM = 64
BRANCH_W = 256
N_HEADS = 4
A_COLS = 3 * BRANCH_W + 4 * N_HEADS
B_COLS = 3 * BRANCH_W
C_COLS = 2 * BRANCH_W
D_COLS = 2 * BRANCH_W
IN_COLS = A_COLS + B_COLS + C_COLS + D_COLS + D_MODEL
CONF_CONV = 31
GDN_CHUNK = 64
MLP_CHUNK = 128
EPS = 1e-6

LANES = 128
SUBLANES = 8
ROW_PACK = 16
AB_PAD = LANES
MOD_ROWS = 24
TOKEN_TILE = 256
CONV_PAD = 16
VMEM_LIMIT = 56 * 1024 * 1024
NEG_BIG = -1e30
GDN_INTERLEAVE = 4


def _sigmoid(x):
    return 1.0 / (1.0 + jnp.exp(-x))


def _silu(x):
    return x * _sigmoid(x)


def _softplus(x):
    return jnp.maximum(x, 0.0) + jnp.log1p(jnp.exp(-jnp.abs(x)))


def _dot(a, b):
    return jnp.dot(a.astype(BF16), b.astype(BF16), preferred_element_type=F32)


def _dot_nt(a, b):
    return lax.dot_general(a.astype(BF16), b.astype(BF16), (((1,), (1,)), ((), ())),
                           preferred_element_type=F32)


def _dot_tn(a, b):
    return lax.dot_general(a.astype(BF16), b.astype(BF16), (((0,), (0,)), ((), ())),
                           preferred_element_type=F32)


def _split3(x):
    x1 = x.astype(BF16)
    r1 = x - x1.astype(F32)
    x2 = r1.astype(BF16)
    x3 = (r1 - x2.astype(F32)).astype(BF16)
    return x1, x2, x3


def _mod_kernel(c_ref, w_ref, b_ref, o_ref):
    s = _silu(c_ref[...])
    o_ref[...] = jnp.dot(s, w_ref[...], preferred_element_type=F32) + b_ref[...]


def _modulation(cvec, w_ada, b_ada):
    depth = w_ada.shape[0]
    return pl.pallas_call(
        _mod_kernel,
        grid=(depth, 3),
        in_specs=[
            pl.BlockSpec((MOD_ROWS, D_MODEL), lambda l, k: (0, 0)),
            pl.BlockSpec((None, D_MODEL, D_MODEL), lambda l, k: (l, 0, k)),
            pl.BlockSpec((None, 1, D_MODEL), lambda l, k: (l, 0, k)),
        ],
        out_specs=pl.BlockSpec((None, MOD_ROWS, D_MODEL), lambda l, k: (l, 0, k)),
        out_shape=jax.ShapeDtypeStruct((depth, MOD_ROWS, 3 * D_MODEL), F32),
        compiler_params=pltpu.CompilerParams(
            dimension_semantics=("arbitrary", "arbitrary"), vmem_limit_bytes=VMEM_LIMIT),
        name="modulation",
    )(cvec, w_ada, b_ada.reshape(depth, 1, 3 * D_MODEL))


def _to_sequence_order(n_rows, n_cols, inverse=False):
    i = lax.broadcasted_iota(jnp.int32, (n_rows * n_cols, n_rows * n_cols), 0)
    j = lax.broadcasted_iota(jnp.int32, (n_rows * n_cols, n_rows * n_cols), 1)
    if inverse:
        i, j = j, i
    return j == (i % n_rows) * n_cols + i // n_rows


def _inproj_kernel(x_ref, sh_ref, sc_ref, g_ref, w_ref, *out_refs, widths):
    col_major = len(x_ref.shape) == 3
    x = x_ref[...].reshape(TOKEN_TILE, D_MODEL) if col_major else x_ref[...]
    ms = jnp.mean(x * x, axis=-1, keepdims=True)
    h = x * lax.rsqrt(ms + EPS) * g_ref[...]
    hb = (h * (1.0 + sc_ref[...]) + sh_ref[...]).astype(BF16)
    if col_major:
        perm = _to_sequence_order(x_ref.shape[0], x_ref.shape[1]).astype(BF16)
        hb = jnp.dot(perm, hb, preferred_element_type=F32).astype(BF16)
    off = 0
    for o_ref, wd in zip(out_refs, widths):
        o_ref[...] = jnp.dot(hb, w_ref[:, off:off + wd], preferred_element_type=F32).astype(o_ref.dtype)
        off += wd


def _mod_spec(layer, kind, ctx):
    if ctx:
        return pl.BlockSpec((None, 1, D_MODEL), lambda b, t: (layer * MOD_ROWS + MOD_ROWS - 8, 0, kind))
    return pl.BlockSpec((None, 1, D_MODEL), lambda b, t: (layer * MOD_ROWS + b, 0, kind))


def _token_spec(bsz, seq, width, col_major):
    if not col_major:
        return (bsz, seq, width), pl.BlockSpec((None, TOKEN_TILE, width), lambda b, t: (b, t, 0))
    rows = seq // GRID_W
    return ((bsz, rows, GRID_W, width),
            pl.BlockSpec((None, rows, TOKEN_TILE // rows, width), lambda b, t: (b, 0, t, 0)))


def _in_projection(x, mod3, norm_g_l, w_l, layer, widths, col_major, ctx):
    bsz, seq, _ = x.shape
    view, x_spec = _token_spec(bsz, seq, D_MODEL, col_major)
    n_cols = sum(widths)
    kern = functools.partial(_inproj_kernel, widths=tuple(widths))
    return pl.pallas_call(
        kern,
        grid=(bsz, seq // TOKEN_TILE),
        in_specs=[x_spec] + [
            _mod_spec(layer, 0, ctx), _mod_spec(layer, 1, ctx),
            pl.BlockSpec((1, D_MODEL), lambda b, t: (0, 0)),
            pl.BlockSpec((D_MODEL, n_cols), lambda b, t: (0, 0)),
        ],
        out_specs=[pl.BlockSpec((None, TOKEN_TILE, wd), lambda b, t: (b, t, 0)) for wd in widths],
        out_shape=[jax.ShapeDtypeStruct((bsz, seq, wd), F32 if wd == AB_PAD else BF16) for wd in widths],
        compiler_params=pltpu.CompilerParams(
            dimension_semantics=("arbitrary", "arbitrary"), vmem_limit_bytes=VMEM_LIMIT),
        name="in_projection",
    )(x.reshape(view), mod3, mod3, norm_g_l.reshape(1, D_MODEL), w_l)


def _outproj_kernel(ya_ref, yb_ref, pg_ref, w_ref, gate_ref, x_ref, fg_ref, o_ref, *, final):
    col_major = len(x_ref.shape) == 3
    sg = _silu(pg_ref[...].astype(F32))
    za = (ya_ref[...].astype(F32) * sg[:, :BRANCH_W]).astype(BF16)
    zb = (yb_ref[...].astype(F32) * sg[:, BRANCH_W:]).astype(BF16)
    if col_major:
        unperm = _to_sequence_order(x_ref.shape[0], x_ref.shape[1], inverse=True).astype(BF16)
        za = jnp.dot(unperm, za, preferred_element_type=F32).astype(BF16)
        zb = jnp.dot(unperm, zb, preferred_element_type=F32).astype(BF16)
    r = (jnp.dot(za, w_ref[:BRANCH_W, :], preferred_element_type=F32)
         + jnp.dot(zb, w_ref[BRANCH_W:, :], preferred_element_type=F32))
    x = x_ref[...].reshape(TOKEN_TILE, D_MODEL) if col_major else x_ref[...]
    xn = x + gate_ref[...] * r
    if final:
        ms = jnp.mean(xn * xn, axis=-1, keepdims=True)
        xn = xn * lax.rsqrt(ms + EPS) * fg_ref[...]
    o_ref[...] = xn.reshape(o_ref.shape)


def _out_projection(y_a, y_bcd, p_gate, w_out_l, mod3, x, layer, col_major, ctx, final_g, final):
    bsz, seq, _ = x.shape
    view, x_spec = _token_spec(bsz, seq, D_MODEL, col_major)
    tok = lambda wd: pl.BlockSpec((None, TOKEN_TILE, wd), lambda b, t: (b, t, 0))
    out = pl.pallas_call(
        functools.partial(_outproj_kernel, final=final),
        grid=(bsz, seq // TOKEN_TILE),
        in_specs=[tok(BRANCH_W), tok(3 * BRANCH_W), tok(D_MODEL),
                  pl.BlockSpec((D_MODEL, D_MODEL), lambda b, t: (0, 0)),
                  _mod_spec(layer, 2, ctx), x_spec,
                  pl.BlockSpec((1, D_MODEL), lambda b, t: (0, 0))],
        out_specs=x_spec,
        out_shape=jax.ShapeDtypeStruct(view, F32),
        compiler_params=pltpu.CompilerParams(
            dimension_semantics=("arbitrary", "arbitrary"), vmem_limit_bytes=VMEM_LIMIT),
        name="out_projection",
    )(y_a, y_bcd, p_gate, w_out_l, mod3, x.reshape(view), final_g.reshape(1, D_MODEL))
    return out.reshape(bsz, seq, D_MODEL)


def _gdn_kernel(pq_ref, pab_ref, cw_ref, alog_ref, dtb_ref, ng_ref, s0f_ref, s0b_ref,
                y_ref, sf_ref, sb_ref,
                qp_ref, op_ref, mp_ref, np_ref, gl_ref, o_ref):
    seq = pq_ref.shape[0]
    n_chunks = seq // GDN_CHUNK
    C = GDN_CHUNK
    W = BRANCH_W
    NG = 4 * N_HEADS

    ri = lax.broadcasted_iota(jnp.int32, (C, W), 0)
    ci = lax.broadcasted_iota(jnp.int32, (C, W), 1)
    cj = ci & (C - 1)
    low_incl = ri >= cj
    low_strict = ri > cj
    up_incl = ri <= cj
    up_strict = ri < cj
    eye_cat = (ri == cj).astype(F32)
    same16 = (ri >> 4) == (cj >> 4)
    level32 = ((ri >> 5) == (cj >> 5)) & ((ri >> 4) != (cj >> 4))
    level64 = (ri >> 5) != (cj >> 5)
    r2 = lax.broadcasted_iota(jnp.int32, (W, W), 0)
    c2 = lax.broadcasted_iota(jnp.int32, (W, W), 1)
    blockmask = (r2 >> 6) == (c2 >> 6)
    ones_blk = blockmask.astype(BF16)
    e_r = lax.broadcasted_iota(jnp.int32, (NG, 4 * W), 0)
    e_c = lax.broadcasted_iota(jnp.int32, (NG, 4 * W), 1)
    expand = (e_r == ((e_c >> 8) * N_HEADS + ((e_c & (W - 1)) >> 6))).astype(BF16)
    lane = lax.broadcasted_iota(jnp.int32, (C, AB_PAD), 1)
    rown = lax.broadcasted_iota(jnp.int32, (C, AB_PAD), 0)
    row64 = lax.broadcasted_iota(jnp.int32, (C, 3 * W), 0)

    def tile_heads(x):
        return jnp.concatenate([x] * N_HEADS, axis=0)

    def bd(x):
        return jnp.where(blockmask, tile_heads(x), 0.0)

    def fold(m):
        m = jnp.where(blockmask, m, 0.0)
        return m[:C] + m[C:2 * C] + m[2 * C:3 * C] + m[3 * C:]

    def cat_mm(a_cat, b_cat):
        return _dot(a_cat, bd(b_cat))

    def head_sums(xs):
        hi = [x.astype(BF16) for x in xs]
        lo = [(x - h.astype(F32)).astype(BF16) for x, h in zip(xs, hi)]
        r = jnp.dot(jnp.concatenate(hi + lo, axis=0), ones_blk, preferred_element_type=F32)
        n, m = len(xs), xs[0].shape[0]
        return [r[i * m:(i + 1) * m] + r[(n + i) * m:(n + i + 1) * m] for i in range(n)]

    def chunk_cumsum(x, reverse):
        s = 1
        while s < C:
            if reverse:
                x = x + jnp.where(rown < C - s, pltpu.roll(x, C - s, 0), 0.0)
            else:
                x = x + jnp.where(rown >= s, pltpu.roll(x, s, 0), 0.0)
            s *= 2
        return x

    cw = cw_ref[...]

    def conv_act(c):
        r0 = pl.multiple_of(c * C, C)
        cur = pq_ref[pl.ds(r0, C), :].astype(F32)
        prev = pq_ref[pl.ds(pl.multiple_of(jnp.maximum(r0 - ROW_PACK, 0), ROW_PACK), ROW_PACK), :].astype(F32)
        nxt = pq_ref[pl.ds(pl.multiple_of(jnp.minimum(r0 + C, seq - ROW_PACK), ROW_PACK), ROW_PACK), :].astype(F32)
        prev_row = prev[ROW_PACK - 1:ROW_PACK, :] * jnp.where(c > 0, 1.0, 0.0)
        next_row = nxt[0:1, :] * jnp.where(c < n_chunks - 1, 1.0, 0.0)
        up = jnp.where(row64 == 0, prev_row, pltpu.roll(cur, 1, 0))
        dn = jnp.where(row64 == C - 1, next_row, pltpu.roll(cur, C - 1, 0))
        return _silu(up * cw[0:1, :] + cur * cw[1:2, :] + dn * cw[2:3, :])

    def gate_sums(c):
        ab = pab_ref[pl.ds(pl.multiple_of(c * C, C), C), :]
        gk = -jnp.exp(alog_ref[...]) * _softplus(ab + dtb_ref[...])
        gb = jnp.where(lane < 2 * N_HEADS, gk, _sigmoid(ab))
        nar = jnp.where(lane < N_HEADS, chunk_cumsum(gb, False),
                        jnp.where(lane < 2 * N_HEADS, chunk_cumsum(gb, True), gb))
        return _split3(nar[:, :NG])

    def parallel_pass(i, carry):
        G = GDN_INTERLEAVE
        cs = [i * G + j for j in range(G)]
        chains = [(j, d) for j in range(G) for d in range(2)]
        acts = [conv_act(c) for c in cs]
        nars = [gate_sums(c) for c in cs]
        sums = [head_sums([a[:, :W] * a[:, :W], a[:, W:2 * W] * a[:, W:2 * W]]) for a in acts]
        q = [a[:, :W] * lax.rsqrt(s[0] + EPS) * (HEAD_DIM ** -0.5) for a, s in zip(acts, sums)]
        k = [a[:, W:2 * W] * lax.rsqrt(s[1] + EPS) for a, s in zip(acts, sums)]
        v = [a[:, 2 * W:] for a in acts]
        gx = [sum(jnp.dot(n, expand, preferred_element_type=F32) for n in n3) for n3 in nars]
        prod = [_dot_nt(jnp.concatenate([q[j], k[j]], axis=0), bd(k[j])) for j in range(G)]
        gi = [gx[j][:, d * W:(d + 1) * W] for j, d in chains]
        beta = [gx[j][:, (2 + d) * W:(3 + d) * W] for j, d in chains]
        incl = [up_incl if d else low_incl for _, d in chains]
        strict = [up_strict if d else low_strict for _, d in chains]
        g_end = [g[0:1, :] if d else g[C - 1:C, :] for g, (_, d) in zip(gi, chains)]
        gj = [jnp.sum(g * eye_cat, axis=0, keepdims=True) for g in gi]
        decay = [jnp.exp(jnp.where(m, g - g2, NEG_BIG)) for m, g, g2 in zip(incl, gi, gj)]
        eg = [jnp.exp(g) for g in gi]
        lmat = [-jnp.where(m, prod[j][C:] * b * dc, 0.0)
                for m, (j, _), b, dc in zip(strict, chains, beta, decay)]
        lblk = [jnp.where(same16, l, 0.0) for l in lmat]
        tinv = [eye_cat + l for l in lblk]
        power = [cat_mm(l, l) for l in lblk]
        for _ in range(2):
            st = [cat_mm(jnp.concatenate([t, p], axis=0), p) for t, p in zip(tinv, power)]
            tinv = [t + s[:C] for t, s in zip(tinv, st)]
            power = [s[C:] for s in st]
        tinv = [t + cat_mm(t, p) for t, p in zip(tinv, power)]
        for level in (level32, level64):
            y = [cat_mm(jnp.where(level, l, 0.0), t) for l, t in zip(lmat, tinv)]
            tinv = [t + cat_mm(t, x) for t, x in zip(tinv, y)]
        tinv = [t.astype(BF16) for t in tinv]
        u = [jnp.dot(t, bd(v[j] * b).astype(BF16), preferred_element_type=F32)
             for t, (j, _), b in zip(tinv, chains, beta)]
        w = [jnp.dot(t, bd(k[j] * b * e).astype(BF16), preferred_element_type=F32)
             for t, (j, _), b, e in zip(tinv, chains, beta, eg)]
        attn = [jnp.where(m, prod[j][:C] * dc, 0.0).astype(BF16) for m, (j, _), dc in zip(incl, chains, decay)]
        au = [jnp.dot(a, bd(x).astype(BF16), preferred_element_type=F32) for a, x in zip(attn, u)]
        aw = [jnp.dot(a, bd(x).astype(BF16), preferred_element_type=F32) for a, x in zip(attn, w)]
        kdec = [k[j] * jnp.exp(ge - g) for (j, _), ge, g in zip(chains, g_end, gi)]
        kw = [_dot_tn(kd, x) for kd, x in zip(kdec, w)]
        ku = [_dot_tn(kd, x) for kd, x in zip(kdec, u)]
        for n, (j, d) in enumerate(chains):
            rows = pl.ds(pl.multiple_of(cs[j] * C, C), C)
            qp_ref[d, rows, :] = (q[j] * eg[n] - aw[n]).astype(BF16)
            op_ref[d, rows, :] = au[n]
            mp_ref[d, rows, :] = (-fold(kw[n])).astype(BF16)
            np_ref[d, rows, :] = fold(ku[n])
            gl_ref[d, pl.ds(pl.multiple_of(cs[j] * SUBLANES, SUBLANES), SUBLANES), :] = jnp.broadcast_to(
                jnp.exp(g_end[n]), (SUBLANES, W))
        return carry

    lax.fori_loop(0, n_chunks // GDN_INTERLEAVE, parallel_pass, 0)

    maskb = blockmask.astype(BF16)

    def scan_step(d, c, s):
        rows = pl.ds(pl.multiple_of(c * C, C), C)
        lhs = jnp.concatenate([qp_ref[d, rows, :], tile_heads(mp_ref[d, rows, :]) * maskb], axis=0)
        t = jnp.dot(lhs, s.astype(BF16), preferred_element_type=F32)
        o = t[:C] + op_ref[d, rows, :]
        gl = gl_ref[d, pl.ds(pl.multiple_of(c * SUBLANES, SUBLANES), 1), :]
        s = s * gl + t[C:] + bd(np_ref[d, rows, :])
        return o, s

    def scan(i, carry):
        s_f, s_b = carry
        o_f, s_f = scan_step(0, i, s_f)
        o_ref[0, pl.ds(pl.multiple_of(i * C, C), C), :] = o_f
        cb = n_chunks - 1 - i
        o_b, s_b = scan_step(1, cb, s_b)
        o_ref[1, pl.ds(pl.multiple_of(cb * C, C), C), :] = o_b
        return s_f, s_b

    s_f, s_b = lax.fori_loop(0, n_chunks, scan, (s0f_ref[...], s0b_ref[...]))
    sf_ref[...] = s_f
    sb_ref[...] = s_b

    def finish(c, carry):
        rows = pl.ds(pl.multiple_of(c * W, W), W)
        o = o_ref[0, rows, :] + o_ref[1, rows, :]
        ms = head_sums([o * o])[0] * (1.0 / HEAD_DIM)
        y_ref[rows, :] = (o * lax.rsqrt(ms + EPS) * ng_ref[...]).astype(y_ref.dtype)
        return carry

    lax.fori_loop(0, seq // W, finish, 0)


def _gdn_mixer(p_qkv, p_ab, conv_w, a_log_l, dt_bias_l, gdn_norm_g_l, s0_f, s0_b):
    bsz, seq, _ = p_qkv.shape
    W = BRANCH_W
    n_chunks = seq // GDN_CHUNK
    pad8 = jnp.zeros((AB_PAD - 2 * N_HEADS,), F32)
    alog_row = jnp.concatenate([a_log_l.reshape(-1), pad8]).reshape(1, AB_PAD)
    dtb_row = jnp.concatenate([dt_bias_l.reshape(-1), pad8]).reshape(1, AB_PAD)
    ng_row = jnp.tile(gdn_norm_g_l, N_HEADS).reshape(1, W)
    per_b = lambda r, wd: pl.BlockSpec((None, r, wd), lambda b: (b, 0, 0))
    const = lambda r, wd: pl.BlockSpec((r, wd), lambda b: (0, 0))
    return pl.pallas_call(
        _gdn_kernel,
        grid=(bsz,),
        in_specs=[per_b(seq, 3 * W), per_b(seq, AB_PAD), const(3, 3 * W), const(1, AB_PAD),
                  const(1, AB_PAD), const(1, W), per_b(W, W), per_b(W, W)],
        out_specs=[per_b(seq, W), per_b(W, W), per_b(W, W)],
        out_shape=[jax.ShapeDtypeStruct((bsz, seq, W), BF16),
                   jax.ShapeDtypeStruct((bsz, W, W), F32),
                   jax.ShapeDtypeStruct((bsz, W, W), F32)],
        scratch_shapes=[
            pltpu.VMEM((2, seq, W), BF16),
            pltpu.VMEM((2, seq, W), F32),
            pltpu.VMEM((2, seq, W), BF16),
            pltpu.VMEM((2, seq, W), F32),
            pltpu.VMEM((2, n_chunks * SUBLANES, W), F32),
            pltpu.VMEM((2, seq, W), F32),
        ],
        compiler_params=pltpu.CompilerParams(
            dimension_semantics=("arbitrary",), vmem_limit_bytes=VMEM_LIMIT),
        name="gdn_mixer",
    )(p_qkv, p_ab, conv_w, alog_row, dtb_row, ng_row, s0_f, s0_b)


def _layernorm(x, g, b):
    mu = jnp.mean(x, axis=-1, keepdims=True)
    xc = x - mu
    var = jnp.mean(xc * xc, axis=-1, keepdims=True)
    return xc * lax.rsqrt(var + EPS) * g + b


def _bcd_kernel(pb_ref, pcd_ref, sw_ref, cw_ref, cb_ref, clg_ref, clb_ref, slg_ref, slb_ref,
                ws_ref, bs_ref, y_ref, z_ref, vln_ref):
    seq = pb_ref.shape[0]
    W = BRANCH_W
    T = GDN_CHUNK
    n_tiles = seq // T
    row = lax.broadcasted_iota(jnp.int32, (T, W), 0)
    sw = sw_ref[...]
    cw = cw_ref[...]

    z_ref[0:CONV_PAD, :] = jnp.zeros((CONV_PAD, W), F32)
    z_ref[CONV_PAD + seq:, :] = jnp.zeros((CONV_PAD, W), F32)

    def stage(t, carry):
        r0 = pl.multiple_of(t * T, T)
        cd = pcd_ref[pl.ds(r0, T), :].astype(F32)
        z_ref[pl.ds(pl.multiple_of(r0 + CONV_PAD, SUBLANES), T), :] = cd[:, :W] * _sigmoid(cd[:, W:2 * W])
        vln_ref[pl.ds(r0, T), :] = _layernorm(cd[:, 3 * W:], slg_ref[...], slb_ref[...])
        return carry

    lax.fori_loop(0, n_tiles, stage, 0)

    def convs(t, carry):
        r0 = pl.multiple_of(t * T, T)
        pb = pb_ref[pl.ds(r0, T), :].astype(F32)
        cur = pb[:, W:2 * W] * pb[:, 2 * W:]
        pv = pb_ref[pl.ds(pl.multiple_of(jnp.maximum(r0 - ROW_PACK, 0), ROW_PACK), ROW_PACK), :].astype(F32)
        nx = pb_ref[pl.ds(pl.multiple_of(jnp.minimum(r0 + T, seq - ROW_PACK), ROW_PACK), ROW_PACK), :].astype(F32)
        prev_row = (pv[:, W:2 * W] * pv[:, 2 * W:])[ROW_PACK - 1:ROW_PACK, :] * jnp.where(t > 0, 1.0, 0.0)
        next_row = (nx[:, W:2 * W] * nx[:, 2 * W:])[0:1, :] * jnp.where(t < n_tiles - 1, 1.0, 0.0)
        up = jnp.where(row == 0, prev_row, pltpu.roll(cur, 1, 0))
        dn = jnp.where(row == T - 1, next_row, pltpu.roll(cur, T - 1, 0))
        y_ref[pl.ds(r0, T), :W] = (pb[:, :W] * (up * sw[0:1, :] + cur * sw[1:2, :] + dn * sw[2:3, :])
                                   ).astype(y_ref.dtype)
        acc = jnp.zeros((T, W), F32)
        for r in range(SUBLANES):
            part = None
            for a in range(4):
                o = SUBLANES * a + r
                if o < 1 or o > CONF_CONV:
                    continue
                win = z_ref[pl.ds(pl.multiple_of(r0 + SUBLANES * a, SUBLANES), T + SUBLANES), :]
                term = win * cw[o - 1:o, :]
                part = term if part is None else part + term
            if r:
                part = pltpu.roll(part, T + SUBLANES - r, 0)
            acc = acc + part[:T, :]
        zc = _layernorm(acc + cb_ref[...], clg_ref[...], clb_ref[...])
        y_ref[pl.ds(r0, T), W:2 * W] = _silu(zc).astype(y_ref.dtype)
        return carry

    lax.fori_loop(0, n_tiles, convs, 0)

    lane_grp = lax.broadcasted_iota(jnp.int32, (MLP_CHUNK, W), 1) >> 6
    wsb = ws_ref[...].astype(BF16)

    def smlp(n, carry):
        r0 = pl.multiple_of(n * MLP_CHUNK, MLP_CHUNK)
        full = jnp.dot(wsb, vln_ref[pl.ds(r0, MLP_CHUNK), :].astype(BF16), preferred_element_type=F32)
        mixed = bs_ref[...]
        for g in range(N_HEADS):
            mixed = mixed + jnp.where(lane_grp == g, full[g * MLP_CHUNK:(g + 1) * MLP_CHUNK, :], 0.0)
        u = pcd_ref[pl.ds(r0, MLP_CHUNK), 2 * W:3 * W].astype(F32)
        y_ref[pl.ds(r0, MLP_CHUNK), 2 * W:] = (u * mixed).astype(y_ref.dtype)
        return carry

    lax.fori_loop(0, seq // MLP_CHUNK, smlp, 0)


def _bcd_mixer(p_b, p_cd, short_w, conf_w, conf_b, conf_g, conf_beta, smlp_g, smlp_beta, smlp_w, smlp_b):
    bsz, seq, _ = p_b.shape
    W = BRANCH_W
    ws_rows = smlp_w.reshape(N_HEADS * MLP_CHUNK, MLP_CHUNK)
    bs_exp = jnp.repeat(smlp_b.T, HEAD_DIM, axis=1)
    row = lambda a: a.reshape(1, W)
    per_b = lambda wd: pl.BlockSpec((None, seq, wd), lambda b: (b, 0, 0))
    const = lambda r, wd: pl.BlockSpec((r, wd), lambda b: (0, 0))
    return pl.pallas_call(
        _bcd_kernel,
        grid=(bsz,),
        in_specs=[per_b(3 * W), per_b(4 * W), const(3, W), const(CONF_CONV, W), const(1, W), const(1, W),
                  const(1, W), const(1, W), const(1, W), const(N_HEADS * MLP_CHUNK, MLP_CHUNK),
                  const(MLP_CHUNK, W)],
        out_specs=per_b(3 * W),
        out_shape=jax.ShapeDtypeStruct((bsz, seq, 3 * W), BF16),
        scratch_shapes=[pltpu.VMEM((seq + 2 * CONV_PAD, W), F32), pltpu.VMEM((seq, W), F32)],
        compiler_params=pltpu.CompilerParams(
            dimension_semantics=("arbitrary",), vmem_limit_bytes=VMEM_LIMIT),
        name="bcd_mixer",
    )(p_b, p_cd, short_w, conf_w, row(conf_b), row(conf_g), row(conf_beta), row(smlp_g), row(smlp_beta),
      ws_rows, bs_exp)


def _permute_w_in(w_in):
    depth = w_in.shape[0]
    b0 = A_COLS
    c0 = b0 + B_COLS
    g0 = c0 + C_COLS + D_COLS
    w_in = w_in.astype(BF16)
    pad = jnp.zeros((depth, D_MODEL, AB_PAD - 4 * N_HEADS), BF16)
    return jnp.concatenate(
        [w_in[:, :, :3 * BRANCH_W], w_in[:, :, b0:c0], w_in[:, :, c0:g0], w_in[:, :, g0:],
         w_in[:, :, 3 * BRANCH_W:A_COLS], pad], axis=-1)


FULL_WIDTHS = (3 * BRANCH_W, 3 * BRANCH_W, 4 * BRANCH_W, D_MODEL, AB_PAD)


def kernel(x, c, ctx, c_ctx, norm_g, w_ada, b_ada, w_in, qkv_conv_w, a_log, dt_bias, gdn_norm_g,
           short_conv_w, conf_conv_w, conf_conv_b, conf_ln_g, conf_ln_b, smlp_ln_g, smlp_ln_b,
           smlp_w, smlp_b, w_out, final_g):
    bsz, seq, _ = x.shape
    depth = w_in.shape[0]
    assert bsz + 1 <= MOD_ROWS - 7 and seq % TOKEN_TILE == 0 and ctx.shape[1] % TOKEN_TILE == 0

    cvec = jnp.zeros((MOD_ROWS, D_MODEL), F32).at[:bsz].set(c).at[MOD_ROWS - 8].set(c_ctx)
    mod3 = _modulation(cvec, w_ada, b_ada).reshape(depth * MOD_ROWS, 1, 3 * D_MODEL)
    w_perm = _permute_w_in(w_in)
    w_out_b = w_out.astype(BF16)
    s_zero = jnp.zeros((bsz, BRANCH_W, BRANCH_W), F32)
    qkv_only = 3 * BRANCH_W

    xc = ctx
    for l in range(depth):
        last = l == depth - 1
        col_major = l % 2 == 1
        mix_args = (short_conv_w[l], conf_conv_w[l], conf_conv_b[l], conf_ln_g[l], conf_ln_b[l],
                    smlp_ln_g[l], smlp_ln_b[l], smlp_w[l], smlp_b[l])
        if last:
            w_ctx = jnp.concatenate([w_perm[l][:, :qkv_only], w_perm[l][:, -AB_PAD:]], axis=-1)
            pc_qkv, pc_ab = _in_projection(xc, mod3, norm_g[l], w_ctx, l, (qkv_only, AB_PAD), False, True)
        else:
            pc_qkv, pc_b, pc_cd, pc_gate, pc_ab = _in_projection(
                xc, mod3, norm_g[l], w_perm[l], l, FULL_WIDTHS, False, True)
        yc_a, s_f, s_b = _gdn_mixer(pc_qkv, pc_ab, qkv_conv_w[l], a_log[l], dt_bias[l], gdn_norm_g[l],
                                    s_zero, s_zero)
        p_qkv, p_b, p_cd, p_gate, p_ab = _in_projection(
            x, mod3, norm_g[l], w_perm[l], l, FULL_WIDTHS, col_major, False)
        y_a, _, _ = _gdn_mixer(p_qkv, p_ab, qkv_conv_w[l], a_log[l], dt_bias[l], gdn_norm_g[l], s_f, s_b)
        y_bcd = _bcd_mixer(p_b, p_cd, *mix_args)
        x = _out_projection(y_a, y_bcd, p_gate, w_out_b[l], mod3, x, l, col_major, False, final_g, last)
        if not last:
            yc_bcd = _bcd_mixer(pc_b, pc_cd, *mix_args)
            xc = _out_projection(yc_a, yc_bcd, pc_gate, w_out_b[l], mod3, xc, l, False, True, final_g, False)
    return x
```

```python
import functools

import jax
import jax.numpy as jnp
from jax import lax
from jax.experimental import pallas as pl
from jax.experimental.pallas import tpu as pltpu

F32 = jnp.float32
BF16 = jnp.bfloat16

D_MODEL = 1024
DEPTH = 4
GRID_W = 64
HEAD_DIM = 64
BRANCH_W = 256
N_HEADS = 4
A_COLS = 3 * BRANCH_W + 4 * N_HEADS
B_COLS = 3 * BRANCH_W
C_COLS = 2 * BRANCH_W
D_COLS = 2 * BRANCH_W
IN_COLS = A_COLS + B_COLS + C_COLS + D_COLS + D_MODEL
CONF_CONV = 31
GDN_CHUNK = 64
MLP_CHUNK = 128
EPS = 1e-6

LANES = 128
SUBLANES = 8
ROW_PACK = 16
AB_PAD = LANES
MOD_ROWS = 24
TOKEN_TILE = 512
PERM_COLS = 8
CONV_PAD = 16
VMEM_LIMIT = 56 * 1024 * 1024
NEG_BIG = -1e30
GDN_INTERLEAVE = 4


def _sigmoid(x):
    return 1.0 / (1.0 + jnp.exp(-x))


def _silu(x):
    return x * _sigmoid(x)


def _softplus(x):
    return jnp.maximum(x, 0.0) + jnp.log1p(jnp.exp(-jnp.abs(x)))


def _dot(a, b):
    return jnp.dot(a.astype(BF16), b.astype(BF16), preferred_element_type=F32)


def _dot_nt(a, b):
    return lax.dot_general(a.astype(BF16), b.astype(BF16), (((1,), (1,)), ((), ())),
                           preferred_element_type=F32)


def _dot_tn(a, b):
    return lax.dot_general(a.astype(BF16), b.astype(BF16), (((0,), (0,)), ((), ())),
                           preferred_element_type=F32)


def _split3(x):
    x1 = x.astype(BF16)
    r1 = x - x1.astype(F32)
    x2 = r1.astype(BF16)
    x3 = (r1 - x2.astype(F32)).astype(BF16)
    return x1, x2, x3


def _mod_kernel(c_ref, w_ref, b_ref, o_ref):
    s = _silu(c_ref[...])
    o_ref[...] = jnp.dot(s, w_ref[...], preferred_element_type=F32) + b_ref[...]


def _modulation(cvec, w_ada, b_ada):
    depth = w_ada.shape[0]
    return pl.pallas_call(
        _mod_kernel,
        grid=(depth, 3),
        in_specs=[
            pl.BlockSpec((MOD_ROWS, D_MODEL), lambda l, k: (0, 0)),
            pl.BlockSpec((None, D_MODEL, D_MODEL), lambda l, k: (l, 0, k)),
            pl.BlockSpec((None, 1, D_MODEL), lambda l, k: (l, 0, k)),
        ],
        out_specs=pl.BlockSpec((None, MOD_ROWS, D_MODEL), lambda l, k: (l, 0, k)),
        out_shape=jax.ShapeDtypeStruct((depth, MOD_ROWS, 3 * D_MODEL), F32),
        compiler_params=pltpu.CompilerParams(
            dimension_semantics=("arbitrary", "arbitrary"), vmem_limit_bytes=VMEM_LIMIT),
        name="modulation",
    )(cvec, w_ada, b_ada.reshape(depth, 1, 3 * D_MODEL))


def _to_sequence_order(n_rows, n_cols, inverse=False):
    i = lax.broadcasted_iota(jnp.int32, (n_rows * n_cols, n_rows * n_cols), 0)
    j = lax.broadcasted_iota(jnp.int32, (n_rows * n_cols, n_rows * n_cols), 1)
    if inverse:
        i, j = j, i
    return j == (i % n_rows) * n_cols + i // n_rows


def _column_groups(ref):
    return [(slice(None), slice(g * PERM_COLS, (g + 1) * PERM_COLS), slice(None))
            for g in range(ref.shape[1] // PERM_COLS)]


def _inproj_kernel(x_ref, sh_ref, sc_ref, g_ref, w_ref, *out_refs, widths):
    def modulated(x):
        ms = jnp.mean(x * x, axis=-1, keepdims=True)
        h = x * lax.rsqrt(ms + EPS) * g_ref[...]
        return (h * (1.0 + sc_ref[...]) + sh_ref[...]).astype(BF16)

    if len(x_ref.shape) == 3:
        rows = x_ref.shape[0]
        perm = _to_sequence_order(rows, PERM_COLS).astype(BF16)
        hb = jnp.concatenate(
            [jnp.dot(perm, modulated(x_ref[idx].reshape(rows * PERM_COLS, D_MODEL)),
                     preferred_element_type=F32).astype(BF16) for idx in _column_groups(x_ref)], axis=0)
    else:
        hb = modulated(x_ref[...])
    off = 0
    for o_ref, wd in zip(out_refs, widths):
        o_ref[...] = jnp.dot(hb, w_ref[:, off:off + wd], preferred_element_type=F32).astype(o_ref.dtype)
        off += wd


def _mod_spec(layer, kind, ctx):
    if ctx:
        return pl.BlockSpec((None, 1, D_MODEL), lambda b, t: (layer * MOD_ROWS + MOD_ROWS - 8, 0, kind))
    return pl.BlockSpec((None, 1, D_MODEL), lambda b, t: (layer * MOD_ROWS + b, 0, kind))


def _token_spec(bsz, seq, width, col_major):
    tile = min(TOKEN_TILE, seq)
    if not col_major:
        return tile, (bsz, seq, width), pl.BlockSpec((None, tile, width), lambda b, t: (b, t, 0))
    rows = seq // GRID_W
    assert tile % (rows * PERM_COLS) == 0
    return (tile, (bsz, rows, GRID_W, width),
            pl.BlockSpec((None, rows, tile // rows, width), lambda b, t: (b, 0, t, 0)))


def _in_projection(x, mod3, norm_g_l, w_l, layer, widths, col_major, ctx):
    bsz, seq, _ = x.shape
    tile, view, x_spec = _token_spec(bsz, seq, D_MODEL, col_major)
    n_cols = sum(widths)
    kern = functools.partial(_inproj_kernel, widths=tuple(widths))
    return pl.pallas_call(
        kern,
        grid=(bsz, seq // tile),
        in_specs=[x_spec] + [
            _mod_spec(layer, 0, ctx), _mod_spec(layer, 1, ctx),
            pl.BlockSpec((1, D_MODEL), lambda b, t: (0, 0)),
            pl.BlockSpec((D_MODEL, n_cols), lambda b, t: (0, 0)),
        ],
        out_specs=[pl.BlockSpec((None, tile, wd), lambda b, t: (b, t, 0)) for wd in widths],
        out_shape=[jax.ShapeDtypeStruct((bsz, seq, wd), F32 if wd == AB_PAD else BF16) for wd in widths],
        compiler_params=pltpu.CompilerParams(
            dimension_semantics=("arbitrary", "arbitrary"), vmem_limit_bytes=VMEM_LIMIT),
        name="in_projection",
    )(x.reshape(view), mod3, mod3, norm_g_l.reshape(1, D_MODEL), w_l)


def _outproj_kernel(y_ref, pg_ref, w_ref, gate_ref, x_ref, fg_ref, o_ref, *, final):
    def residual(x, r):
        xn = x + gate_ref[...] * r
        if final:
            ms = jnp.mean(xn * xn, axis=-1, keepdims=True)
            xn = xn * lax.rsqrt(ms + EPS) * fg_ref[...]
        return xn

    z = (y_ref[...].astype(F32) * _silu(pg_ref[...].astype(F32))).astype(BF16)
    if len(x_ref.shape) == 3:
        rows = x_ref.shape[0]
        n = rows * PERM_COLS
        unperm = _to_sequence_order(rows, PERM_COLS, inverse=True).astype(BF16)
        groups = _column_groups(x_ref)
        z = jnp.concatenate(
            [jnp.dot(unperm, z[g * n:(g + 1) * n, :], preferred_element_type=F32).astype(BF16)
             for g in range(len(groups))], axis=0)
        r = jnp.dot(z, w_ref[...], preferred_element_type=F32)
        for g, idx in enumerate(groups):
            xn = residual(x_ref[idx].reshape(n, D_MODEL), r[g * n:(g + 1) * n, :])
            o_ref[idx] = xn.reshape(rows, PERM_COLS, D_MODEL)
    else:
        o_ref[...] = residual(x_ref[...], jnp.dot(z, w_ref[...], preferred_element_type=F32))


def _out_projection(y, p_gate, w_out_l, mod3, x, layer, col_major, ctx, final_g, final):
    bsz, seq, _ = x.shape
    tile, view, x_spec = _token_spec(bsz, seq, D_MODEL, col_major)
    tok = pl.BlockSpec((None, tile, D_MODEL), lambda b, t: (b, t, 0))
    out = pl.pallas_call(
        functools.partial(_outproj_kernel, final=final),
        grid=(bsz, seq // tile),
        in_specs=[tok, tok,
                  pl.BlockSpec((D_MODEL, D_MODEL), lambda b, t: (0, 0)),
                  _mod_spec(layer, 2, ctx), x_spec,
                  pl.BlockSpec((1, D_MODEL), lambda b, t: (0, 0))],
        out_specs=x_spec,
        out_shape=jax.ShapeDtypeStruct(view, F32),
        compiler_params=pltpu.CompilerParams(
            dimension_semantics=("arbitrary", "arbitrary"), vmem_limit_bytes=VMEM_LIMIT),
        name="out_projection",
    )(y, p_gate, w_out_l, mod3, x.reshape(view), final_g.reshape(1, D_MODEL))
    return out.reshape(bsz, seq, D_MODEL)


def _mixer_kernel(*refs, bcd):
    pq_ref, pab_ref, cw_ref, alog_ref, dtb_ref, ng_ref, s0f_ref, s0b_ref = refs[:8]
    n_in = 8 + (N_BCD_INPUTS if bcd else 0)
    y_ref, sf_ref, sb_ref, qp_ref, op_ref, mp_ref, np_ref, gl_ref, o_ref = refs[n_in:n_in + 9]
    if bcd:
        bcd_stage, bcd_convs, bcd_smlp = _bcd_stages(*refs[8:n_in], y_ref, *refs[n_in + 9:])
    seq = pq_ref.shape[0]
    n_chunks = seq // GDN_CHUNK
    C = GDN_CHUNK
    W = BRANCH_W
    NG = 4 * N_HEADS

    ri = lax.broadcasted_iota(jnp.int32, (C, W), 0)
    ci = lax.broadcasted_iota(jnp.int32, (C, W), 1)
    cj = ci & (C - 1)
    low_incl = ri >= cj
    low_strict = ri > cj
    up_incl = ri <= cj
    up_strict = ri < cj
    eye_cat = (ri == cj).astype(F32)
    same16 = (ri >> 4) == (cj >> 4)
    level32 = ((ri >> 5) == (cj >> 5)) & ((ri >> 4) != (cj >> 4))
    level64 = (ri >> 5) != (cj >> 5)
    r2 = lax.broadcasted_iota(jnp.int32, (W, W), 0)
    c2 = lax.broadcasted_iota(jnp.int32, (W, W), 1)
    blockmask = (r2 >> 6) == (c2 >> 6)
    ones_blk = blockmask.astype(BF16)
    e_r = lax.broadcasted_iota(jnp.int32, (NG, 4 * W), 0)
    e_c = lax.broadcasted_iota(jnp.int32, (NG, 4 * W), 1)
    expand = (e_r == ((e_c >> 8) * N_HEADS + ((e_c & (W - 1)) >> 6))).astype(BF16)
    lane = lax.broadcasted_iota(jnp.int32, (C, AB_PAD), 1)
    rown = lax.broadcasted_iota(jnp.int32, (C, AB_PAD), 0)
    row64 = lax.broadcasted_iota(jnp.int32, (C, 3 * W), 0)

    def tile_heads(x):
        return jnp.concatenate([x] * N_HEADS, axis=0)

    def bd(x):
        return jnp.where(blockmask, tile_heads(x), 0.0)

    def fold(m):
        m = jnp.where(blockmask, m, 0.0)
        return m[:C] + m[C:2 * C] + m[2 * C:3 * C] + m[3 * C:]

    def cat_mm(a_cat, b_cat):
        return _dot(a_cat, bd(b_cat))

    def head_sums(xs):
        hi = [x.astype(BF16) for x in xs]
        lo = [(x - h.astype(F32)).astype(BF16) for x, h in zip(xs, hi)]
        r = jnp.dot(jnp.concatenate(hi + lo, axis=0), ones_blk, preferred_element_type=F32)
        n, m = len(xs), xs[0].shape[0]
        return [r[i * m:(i + 1) * m] + r[(n + i) * m:(n + i + 1) * m] for i in range(n)]

    def chunk_cumsum(x, reverse):
        s = 1
        while s < C:
            if reverse:
                x = x + jnp.where(rown < C - s, pltpu.roll(x, C - s, 0), 0.0)
            else:
                x = x + jnp.where(rown >= s, pltpu.roll(x, s, 0), 0.0)
            s *= 2
        return x

    cw = cw_ref[...]

    def conv_act(c):
        r0 = pl.multiple_of(c * C, C)
        cur = pq_ref[pl.ds(r0, C), :].astype(F32)
        prev = pq_ref[pl.ds(pl.multiple_of(jnp.maximum(r0 - ROW_PACK, 0), ROW_PACK), ROW_PACK), :].astype(F32)
        nxt = pq_ref[pl.ds(pl.multiple_of(jnp.minimum(r0 + C, seq - ROW_PACK), ROW_PACK), ROW_PACK), :].astype(F32)
        prev_row = prev[ROW_PACK - 1:ROW_PACK, :] * jnp.where(c > 0, 1.0, 0.0)
        next_row = nxt[0:1, :] * jnp.where(c < n_chunks - 1, 1.0, 0.0)
        up = jnp.where(row64 == 0, prev_row, pltpu.roll(cur, 1, 0))
        dn = jnp.where(row64 == C - 1, next_row, pltpu.roll(cur, C - 1, 0))
        return _silu(up * cw[0:1, :] + cur * cw[1:2, :] + dn * cw[2:3, :])

    def gate_sums(c):
        ab = pab_ref[pl.ds(pl.multiple_of(c * C, C), C), :]
        gk = -jnp.exp(alog_ref[...]) * _softplus(ab + dtb_ref[...])
        gb = jnp.where(lane < 2 * N_HEADS, gk, _sigmoid(ab))
        nar = jnp.where(lane < N_HEADS, chunk_cumsum(gb, False),
                        jnp.where(lane < 2 * N_HEADS, chunk_cumsum(gb, True), gb))
        return _split3(nar[:, :NG])

    def parallel_pass(i, carry):
        G = GDN_INTERLEAVE
        cs = [i * G + j for j in range(G)]
        chains = [(j, d) for j in range(G) for d in range(2)]
        if bcd:
            for c in cs:
                bcd_stage(c)
        acts = [conv_act(c) for c in cs]
        nars = [gate_sums(c) for c in cs]
        sums = [head_sums([a[:, :W] * a[:, :W], a[:, W:2 * W] * a[:, W:2 * W]]) for a in acts]
        q = [a[:, :W] * lax.rsqrt(s[0] + EPS) * (HEAD_DIM ** -0.5) for a, s in zip(acts, sums)]
        k = [a[:, W:2 * W] * lax.rsqrt(s[1] + EPS) for a, s in zip(acts, sums)]
        v = [a[:, 2 * W:] for a in acts]
        gx = [sum(jnp.dot(n, expand, preferred_element_type=F32) for n in n3) for n3 in nars]
        prod = [_dot_nt(jnp.concatenate([q[j], k[j]], axis=0), bd(k[j])) for j in range(G)]
        gi = [gx[j][:, d * W:(d + 1) * W] for j, d in chains]
        beta = [gx[j][:, (2 + d) * W:(3 + d) * W] for j, d in chains]
        incl = [up_incl if d else low_incl for _, d in chains]
        strict = [up_strict if d else low_strict for _, d in chains]
        g_end = [g[0:1, :] if d else g[C - 1:C, :] for g, (_, d) in zip(gi, chains)]
        gj = [jnp.sum(g * eye_cat, axis=0, keepdims=True) for g in gi]
        decay = [jnp.exp(jnp.where(m, g - g2, NEG_BIG)) for m, g, g2 in zip(incl, gi, gj)]
        eg = [jnp.exp(g) for g in gi]
        lmat = [-jnp.where(m, prod[j][C:] * b * dc, 0.0)
                for m, (j, _), b, dc in zip(strict, chains, beta, decay)]
        lblk = [jnp.where(same16, l, 0.0) for l in lmat]
        tinv = [eye_cat + l for l in lblk]
        power = [cat_mm(l, l) for l in lblk]
        for _ in range(2):
            st = [cat_mm(jnp.concatenate([t, p], axis=0), p) for t, p in zip(tinv, power)]
            tinv = [t + s[:C] for t, s in zip(tinv, st)]
            power = [s[C:] for s in st]
        tinv = [t + cat_mm(t, p) for t, p in zip(tinv, power)]
        for level in (level32, level64):
            y = [cat_mm(jnp.where(level, l, 0.0), t) for l, t in zip(lmat, tinv)]
            tinv = [t + cat_mm(t, x) for t, x in zip(tinv, y)]
        tinv = [t.astype(BF16) for t in tinv]
        u = [jnp.dot(t, bd(v[j] * b).astype(BF16), preferred_element_type=F32)
             for t, (j, _), b in zip(tinv, chains, beta)]
        w = [jnp.dot(t, bd(k[j] * b * e).astype(BF16), preferred_element_type=F32)
             for t, (j, _), b, e in zip(tinv, chains, beta, eg)]
        attn = [jnp.where(m, prod[j][:C] * dc, 0.0).astype(BF16) for m, (j, _), dc in zip(incl, chains, decay)]
        au = [jnp.dot(a, bd(x).astype(BF16), preferred_element_type=F32) for a, x in zip(attn, u)]
        aw = [jnp.dot(a, bd(x).astype(BF16), preferred_element_type=F32) for a, x in zip(attn, w)]
        kdec = [k[j] * jnp.exp(ge - g) for (j, _), ge, g in zip(chains, g_end, gi)]
        kw = [_dot_tn(kd, x) for kd, x in zip(kdec, w)]
        ku = [_dot_tn(kd, x) for kd, x in zip(kdec, u)]
        for n, (j, d) in enumerate(chains):
            rows = pl.ds(pl.multiple_of(cs[j] * C, C), C)
            qp_ref[d, rows, :] = (q[j] * eg[n] - aw[n]).astype(BF16)
            op_ref[d, rows, :] = au[n]
            mp_ref[d, rows, :] = (-fold(kw[n])).astype(BF16)
            np_ref[d, rows, :] = fold(ku[n])
            gl_ref[d, pl.ds(pl.multiple_of(cs[j] * SUBLANES, SUBLANES), SUBLANES), :] = jnp.broadcast_to(
                jnp.exp(g_end[n]), (SUBLANES, W))
        return carry

    lax.fori_loop(0, n_chunks // GDN_INTERLEAVE, parallel_pass, 0)

    maskb = blockmask.astype(BF16)

    def scan_step(d, c, s):
        rows = pl.ds(pl.multiple_of(c * C, C), C)
        lhs = jnp.concatenate([qp_ref[d, rows, :], tile_heads(mp_ref[d, rows, :]) * maskb], axis=0)
        t = jnp.dot(lhs, s.astype(BF16), preferred_element_type=F32)
        o = t[:C] + op_ref[d, rows, :]
        gl = gl_ref[d, pl.ds(pl.multiple_of(c * SUBLANES, SUBLANES), 1), :]
        s = s * gl + t[C:] + bd(np_ref[d, rows, :])
        return o, s

    def scan(i, carry, first_visit):
        s_f, s_b = carry
        o_f, s_f = scan_step(0, i, s_f)
        cb = n_chunks - 1 - i
        o_b, s_b = scan_step(1, cb, s_b)
        rows_f = pl.ds(pl.multiple_of(i * C, C), C)
        rows_b = pl.ds(pl.multiple_of(cb * C, C), C)
        if first_visit:
            o_ref[rows_f, :] = o_f
            o_ref[rows_b, :] = o_b
        else:
            o_ref[rows_f, :] += o_f
            o_ref[rows_b, :] += o_b
        if bcd:
            bcd_convs(i)
        return s_f, s_b

    half = n_chunks // 2
    states = lax.fori_loop(0, half, functools.partial(scan, first_visit=True), (s0f_ref[...], s0b_ref[...]))
    s_f, s_b = lax.fori_loop(half, n_chunks, functools.partial(scan, first_visit=False), states)
    sf_ref[...] = s_f
    sb_ref[...] = s_b

    def finish(c, carry):
        rows = pl.ds(pl.multiple_of(c * W, W), W)
        o = o_ref[rows, :]
        ms = head_sums([o * o])[0] * (1.0 / HEAD_DIM)
        y_ref[rows, :W] = (o * lax.rsqrt(ms + EPS) * ng_ref[...]).astype(y_ref.dtype)
        if bcd:
            for n in range(W // MLP_CHUNK):
                bcd_smlp(c * (W // MLP_CHUNK) + n)
        return carry

    lax.fori_loop(0, seq // W, finish, 0)


def _mixer(p_qkv, p_ab, conv_w, a_log_l, dt_bias_l, gdn_norm_g_l, s0_f, s0_b, bcd_args=None):
    bsz, seq, _ = p_qkv.shape
    W = BRANCH_W
    n_chunks = seq // GDN_CHUNK
    bcd = bcd_args is not None
    pad8 = jnp.zeros((AB_PAD - 2 * N_HEADS,), F32)
    alog_row = jnp.concatenate([a_log_l.reshape(-1), pad8]).reshape(1, AB_PAD)
    dtb_row = jnp.concatenate([dt_bias_l.reshape(-1), pad8]).reshape(1, AB_PAD)
    ng_row = jnp.tile(gdn_norm_g_l, N_HEADS).reshape(1, W)
    per_b = lambda r, wd: pl.BlockSpec((None, r, wd), lambda b: (b, 0, 0))
    const = lambda r, wd: pl.BlockSpec((r, wd), lambda b: (0, 0))
    in_specs = [per_b(seq, 3 * W), per_b(seq, AB_PAD), const(3, 3 * W), const(1, AB_PAD),
                const(1, AB_PAD), const(1, W), per_b(W, W), per_b(W, W)]
    args = [p_qkv, p_ab, conv_w, alog_row, dtb_row, ng_row, s0_f, s0_b]
    scratch = [
        pltpu.VMEM((2, seq, W), BF16),
        pltpu.VMEM((2, seq, W), F32),
        pltpu.VMEM((2, seq, W), BF16),
        pltpu.VMEM((2, seq, W), F32),
        pltpu.VMEM((2, n_chunks * SUBLANES, W), F32),
        pltpu.VMEM((seq, W), F32),
    ]
    if bcd:
        p_b, p_cd, short_w, conf_w, conf_b, conf_g, conf_beta, smlp_g, smlp_beta, smlp_w, smlp_b = bcd_args
        row = lambda a: a.reshape(1, W)
        ws_rows = smlp_w.reshape(N_HEADS * MLP_CHUNK, MLP_CHUNK)
        bs_exp = jnp.repeat(smlp_b.T, HEAD_DIM, axis=1)
        in_specs += [per_b(seq, 3 * W), per_b(seq, 4 * W), const(3, W), const(CONF_CONV, W)] + [const(1, W)] * 5
        in_specs += [const(N_HEADS * MLP_CHUNK, MLP_CHUNK), const(MLP_CHUNK, W)]
        args += [p_b, p_cd, short_w, conf_w, row(conf_b), row(conf_g), row(conf_beta), row(smlp_g),
                 row(smlp_beta), ws_rows, bs_exp]
        assert len(args) == 8 + N_BCD_INPUTS
        scratch += [pltpu.VMEM((seq + 2 * CONV_PAD, W), F32),
                    pltpu.VMEM((seq, W), BF16)]
    y_w = 4 * W if bcd else W
    return pl.pallas_call(
        functools.partial(_mixer_kernel, bcd=bcd),
        grid=(bsz,),
        in_specs=in_specs,
        out_specs=[per_b(seq, y_w), per_b(W, W), per_b(W, W)],
        out_shape=[jax.ShapeDtypeStruct((bsz, seq, y_w), BF16),
                   jax.ShapeDtypeStruct((bsz, W, W), F32),
                   jax.ShapeDtypeStruct((bsz, W, W), F32)],
        scratch_shapes=scratch,
        compiler_params=pltpu.CompilerParams(
            dimension_semantics=("arbitrary",), vmem_limit_bytes=VMEM_LIMIT),
        name="mixer",
    )(*args)


def _layernorm(x, g, b):
    mu = jnp.mean(x, axis=-1, keepdims=True)
    xc = x - mu
    var = jnp.mean(xc * xc, axis=-1, keepdims=True)
    return xc * lax.rsqrt(var + EPS) * g + b


N_BCD_INPUTS = 11


def _bcd_stages(pb_ref, pcd_ref, sw_ref, cw_ref, cb_ref, clg_ref, clb_ref, slg_ref, slb_ref,
                ws_ref, bs_ref, y_ref, z_ref, vln_ref):
    seq = pb_ref.shape[0]
    W = BRANCH_W
    T = GDN_CHUNK
    n_tiles = seq // T
    row = lax.broadcasted_iota(jnp.int32, (T, W), 0)
    sw = sw_ref[...]
    cw = cw_ref[...]

    z_ref[0:CONV_PAD, :] = jnp.zeros((CONV_PAD, W), F32)
    z_ref[CONV_PAD + seq:, :] = jnp.zeros((CONV_PAD, W), F32)

    def stage(t):
        r0 = pl.multiple_of(t * T, T)
        cd = pcd_ref[pl.ds(r0, T), :].astype(F32)
        z_ref[pl.ds(pl.multiple_of(r0 + CONV_PAD, SUBLANES), T), :] = cd[:, :W] * _sigmoid(cd[:, W:2 * W])
        vln_ref[pl.ds(r0, T), :] = _layernorm(cd[:, 3 * W:], slg_ref[...], slb_ref[...]).astype(vln_ref.dtype)

    def convs(t):
        r0 = pl.multiple_of(t * T, T)
        pb = pb_ref[pl.ds(r0, T), :].astype(F32)
        cur = pb[:, W:2 * W] * pb[:, 2 * W:]
        pv = pb_ref[pl.ds(pl.multiple_of(jnp.maximum(r0 - ROW_PACK, 0), ROW_PACK), ROW_PACK), :].astype(F32)
        nx = pb_ref[pl.ds(pl.multiple_of(jnp.minimum(r0 + T, seq - ROW_PACK), ROW_PACK), ROW_PACK), :].astype(F32)
        prev_row = (pv[:, W:2 * W] * pv[:, 2 * W:])[ROW_PACK - 1:ROW_PACK, :] * jnp.where(t > 0, 1.0, 0.0)
        next_row = (nx[:, W:2 * W] * nx[:, 2 * W:])[0:1, :] * jnp.where(t < n_tiles - 1, 1.0, 0.0)
        up = jnp.where(row == 0, prev_row, pltpu.roll(cur, 1, 0))
        dn = jnp.where(row == T - 1, next_row, pltpu.roll(cur, T - 1, 0))
        y_ref[pl.ds(r0, T), W:2 * W] = (pb[:, :W] * (up * sw[0:1, :] + cur * sw[1:2, :] + dn * sw[2:3, :])
                                        ).astype(y_ref.dtype)
        acc = jnp.zeros((T, W), F32)
        for r in range(SUBLANES):
            part = None
            for a in range(4):
                o = SUBLANES * a + r
                if o < 1 or o > CONF_CONV:
                    continue
                win = z_ref[pl.ds(pl.multiple_of(r0 + SUBLANES * a, SUBLANES), T + SUBLANES), :]
                term = win * cw[o - 1:o, :]
                part = term if part is None else part + term
            if r:
                part = pltpu.roll(part, T + SUBLANES - r, 0)
            acc = acc + part[:T, :]
        zc = _layernorm(acc + cb_ref[...], clg_ref[...], clb_ref[...])
        y_ref[pl.ds(r0, T), 2 * W:3 * W] = _silu(zc).astype(y_ref.dtype)

    lane_grp = lax.broadcasted_iota(jnp.int32, (MLP_CHUNK, W), 1) >> 6
    wsb = ws_ref[...].astype(BF16)

    def smlp(n):
        r0 = pl.multiple_of(n * MLP_CHUNK, MLP_CHUNK)
        full = jnp.dot(wsb, vln_ref[pl.ds(r0, MLP_CHUNK), :].astype(BF16), preferred_element_type=F32)
        mixed = bs_ref[...]
        for g in range(N_HEADS):
            mixed = mixed + jnp.where(lane_grp == g, full[g * MLP_CHUNK:(g + 1) * MLP_CHUNK, :], 0.0)
        u = pcd_ref[pl.ds(r0, MLP_CHUNK), 2 * W:3 * W].astype(F32)
        y_ref[pl.ds(r0, MLP_CHUNK), 3 * W:] = (u * mixed).astype(y_ref.dtype)

    return stage, convs, smlp


def _permute_w_in(w_in):
    depth = w_in.shape[0]
    b0 = A_COLS
    c0 = b0 + B_COLS
    g0 = c0 + C_COLS + D_COLS
    w_in = w_in.astype(BF16)
    pad = jnp.zeros((depth, D_MODEL, AB_PAD - 4 * N_HEADS), BF16)
    return jnp.concatenate(
        [w_in[:, :, :3 * BRANCH_W], w_in[:, :, b0:c0], w_in[:, :, c0:g0], w_in[:, :, g0:],
         w_in[:, :, 3 * BRANCH_W:A_COLS], pad], axis=-1)


FULL_WIDTHS = (3 * BRANCH_W, 3 * BRANCH_W, 4 * BRANCH_W, D_MODEL, AB_PAD)


def kernel(x, c, ctx, c_ctx, norm_g, w_ada, b_ada, w_in, qkv_conv_w, a_log, dt_bias, gdn_norm_g,
           short_conv_w, conf_conv_w, conf_conv_b, conf_ln_g, conf_ln_b, smlp_ln_g, smlp_ln_b,
           smlp_w, smlp_b, w_out, final_g):
    bsz, seq, _ = x.shape
    depth = w_in.shape[0]
    assert bsz + 1 <= MOD_ROWS - 7 and seq % TOKEN_TILE == 0

    cvec = jnp.zeros((MOD_ROWS, D_MODEL), F32).at[:bsz].set(c).at[MOD_ROWS - 8].set(c_ctx)
    mod3 = _modulation(cvec, w_ada, b_ada).reshape(depth * MOD_ROWS, 1, 3 * D_MODEL)
    w_perm = _permute_w_in(w_in)
    w_out_b = w_out.astype(BF16)
    s_zero = jnp.zeros((bsz, BRANCH_W, BRANCH_W), F32)
    qkv_only = 3 * BRANCH_W

    xc = ctx
    for l in range(depth):
        last = l == depth - 1
        col_major = l % 2 == 1
        mix_args = (short_conv_w[l], conf_conv_w[l], conf_conv_b[l], conf_ln_g[l], conf_ln_b[l],
                    smlp_ln_g[l], smlp_ln_b[l], smlp_w[l], smlp_b[l])
        if last:
            w_ctx = jnp.concatenate([w_perm[l][:, :qkv_only], w_perm[l][:, -AB_PAD:]], axis=-1)
            pc_qkv, pc_ab = _in_projection(xc, mod3, norm_g[l], w_ctx, l, (qkv_only, AB_PAD), False, True)
        else:
            pc_qkv, pc_b, pc_cd, pc_gate, pc_ab = _in_projection(
                xc, mod3, norm_g[l], w_perm[l], l, FULL_WIDTHS, False, True)
        gdn_args = (qkv_conv_w[l], a_log[l], dt_bias[l], gdn_norm_g[l])
        yc, s_f, s_b = _mixer(pc_qkv, pc_ab, *gdn_args, s_zero, s_zero,
                              None if last else (pc_b, pc_cd) + mix_args)
        p_qkv, p_b, p_cd, p_gate, p_ab = _in_projection(
            x, mod3, norm_g[l], w_perm[l], l, FULL_WIDTHS, col_major, False)
        y, _, _ = _mixer(p_qkv, p_ab, *gdn_args, s_f, s_b, (p_b, p_cd) + mix_args)
        x = _out_projection(y, p_gate, w_out_b[l], mod3, x, l, col_major, False, final_g, last)
        if not last:
            xc = _out_projection(yc, pc_gate, w_out_b[l], mod3, xc, l, False, True, final_g, False)
    return x
```

```python
import functools

import jax
import jax.numpy as jnp
from jax import lax
from jax.experimental import pallas as pl
from jax.experimental.pallas import tpu as pltpu

F32 = jnp.float32
BF16 = jnp.bfloat16

D_MODEL = 1024
DEPTH = 4
GRID_W = 64
HEAD_DIM = 64
BRANCH_W = 256
N_HEADS = 4
A_COLS = 3 * BRANCH_W + 4 * N_HEADS
B_COLS = 3 * BRANCH_W
C_COLS = 2 * BRANCH_W
D_COLS = 2 * BRANCH_W
IN_COLS = A_COLS + B_COLS + C_COLS + D_COLS + D_MODEL
CONF_CONV = 31
GDN_CHUNK = 64
MLP_CHUNK = 128
EPS = 1e-6

LANES = 128
SUBLANES = 8
ROW_PACK = 16
AB_PAD = LANES
MOD_ROWS = 24
TOKEN_TILE = 1024
PERM_COLS = 8
CONV_PAD = 16
VMEM_LIMIT = 56 * 1024 * 1024
NEG_BIG = -1e30
GDN_INTERLEAVE = 4


def _sigmoid(x):
    return 1.0 / (1.0 + jnp.exp(-x))


def _silu(x):
    return x * _sigmoid(x)


def _softplus(x):
    return jnp.maximum(x, 0.0) + jnp.log1p(jnp.exp(-jnp.abs(x)))


def _dot(a, b):
    return jnp.dot(a.astype(BF16), b.astype(BF16), preferred_element_type=F32)


def _dot_nt(a, b):
    return lax.dot_general(a.astype(BF16), b.astype(BF16), (((1,), (1,)), ((), ())),
                           preferred_element_type=F32)


def _dot_tn(a, b):
    return lax.dot_general(a.astype(BF16), b.astype(BF16), (((0,), (0,)), ((), ())),
                           preferred_element_type=F32)


def _split3(x):
    x1 = x.astype(BF16)
    r1 = x - x1.astype(F32)
    x2 = r1.astype(BF16)
    x3 = (r1 - x2.astype(F32)).astype(BF16)
    return x1, x2, x3


def _mod_kernel(c_ref, w_ref, b_ref, o_ref):
    s = _silu(c_ref[...])
    o_ref[...] = jnp.dot(s, w_ref[...], preferred_element_type=F32) + b_ref[...]


def _modulation(cvec, w_ada, b_ada):
    depth = w_ada.shape[0]
    return pl.pallas_call(
        _mod_kernel,
        grid=(depth, 3),
        in_specs=[
            pl.BlockSpec((MOD_ROWS, D_MODEL), lambda l, k: (0, 0)),
            pl.BlockSpec((None, D_MODEL, D_MODEL), lambda l, k: (l, 0, k)),
            pl.BlockSpec((None, 1, D_MODEL), lambda l, k: (l, 0, k)),
        ],
        out_specs=pl.BlockSpec((None, MOD_ROWS, D_MODEL), lambda l, k: (l, 0, k)),
        out_shape=jax.ShapeDtypeStruct((depth, MOD_ROWS, 3 * D_MODEL), F32),
        compiler_params=pltpu.CompilerParams(
            dimension_semantics=("arbitrary", "arbitrary"), vmem_limit_bytes=VMEM_LIMIT),
        name="modulation",
    )(cvec, w_ada, b_ada.reshape(depth, 1, 3 * D_MODEL))


def _to_sequence_order(n_rows, n_cols, inverse=False):
    i = lax.broadcasted_iota(jnp.int32, (n_rows * n_cols, n_rows * n_cols), 0)
    j = lax.broadcasted_iota(jnp.int32, (n_rows * n_cols, n_rows * n_cols), 1)
    if inverse:
        i, j = j, i
    return j == (i % n_rows) * n_cols + i // n_rows


def _column_groups(ref):
    return [(slice(None), slice(g * PERM_COLS, (g + 1) * PERM_COLS), slice(None))
            for g in range(ref.shape[1] // PERM_COLS)]


def _inproj_kernel(x_ref, sh_ref, sc_ref, g_ref, w_ref, *out_refs, widths):
    def modulated(x):
        ms = jnp.mean(x * x, axis=-1, keepdims=True)
        h = x * lax.rsqrt(ms + EPS) * g_ref[...]
        return (h * (1.0 + sc_ref[...]) + sh_ref[...]).astype(BF16)

    if len(x_ref.shape) == 3:
        rows = x_ref.shape[0]
        perm = _to_sequence_order(rows, PERM_COLS).astype(BF16)
        hb = jnp.concatenate(
            [jnp.dot(perm, modulated(x_ref[idx].reshape(rows * PERM_COLS, D_MODEL)),
                     preferred_element_type=F32).astype(BF16) for idx in _column_groups(x_ref)], axis=0)
    else:
        hb = modulated(x_ref[...])
    off = 0
    for o_ref, wd in zip(out_refs, widths):
        o_ref[...] = jnp.dot(hb, w_ref[:, off:off + wd], preferred_element_type=F32).astype(o_ref.dtype)
        off += wd


def _mod_spec(layer, kind, ctx):
    if ctx:
        return pl.BlockSpec((None, 1, D_MODEL), lambda b, t: (layer * MOD_ROWS + MOD_ROWS - 8, 0, kind))
    return pl.BlockSpec((None, 1, D_MODEL), lambda b, t: (layer * MOD_ROWS + b, 0, kind))


def _token_spec(bsz, seq, width, col_major):
    tile = min(TOKEN_TILE, seq)
    if not col_major:
        return tile, (bsz, seq, width), pl.BlockSpec((None, tile, width), lambda b, t: (b, t, 0))
    rows = seq // GRID_W
    assert tile % (rows * PERM_COLS) == 0
    return (tile, (bsz, rows, GRID_W, width),
            pl.BlockSpec((None, rows, tile // rows, width), lambda b, t: (b, 0, t, 0)))


def _in_projection(x, mod3, norm_g_l, w_l, layer, widths, col_major, ctx):
    bsz, seq, _ = x.shape
    tile, view, x_spec = _token_spec(bsz, seq, D_MODEL, col_major)
    n_cols = sum(widths)
    kern = functools.partial(_inproj_kernel, widths=tuple(widths))
    return pl.pallas_call(
        kern,
        grid=(bsz, seq // tile),
        in_specs=[x_spec] + [
            _mod_spec(layer, 0, ctx), _mod_spec(layer, 1, ctx),
            pl.BlockSpec((1, D_MODEL), lambda b, t: (0, 0)),
            pl.BlockSpec((D_MODEL, n_cols), lambda b, t: (0, 0)),
        ],
        out_specs=[pl.BlockSpec((None, tile, wd), lambda b, t: (b, t, 0)) for wd in widths],
        out_shape=[jax.ShapeDtypeStruct((bsz, seq, wd), F32 if wd == AB_PAD else BF16) for wd in widths],
        compiler_params=pltpu.CompilerParams(
            dimension_semantics=("arbitrary", "arbitrary"), vmem_limit_bytes=VMEM_LIMIT),
        name="in_projection",
    )(x.reshape(view), mod3, mod3, norm_g_l.reshape(1, D_MODEL), w_l)


def _outproj_kernel(y_ref, pg_ref, w_ref, gate_ref, x_ref, fg_ref, o_ref, *, final):
    def residual(x, r):
        xn = x + gate_ref[...] * r
        if final:
            ms = jnp.mean(xn * xn, axis=-1, keepdims=True)
            xn = xn * lax.rsqrt(ms + EPS) * fg_ref[...]
        return xn

    z = (y_ref[...].astype(F32) * _silu(pg_ref[...].astype(F32))).astype(BF16)
    if len(x_ref.shape) == 3:
        rows = x_ref.shape[0]
        n = rows * PERM_COLS
        unperm = _to_sequence_order(rows, PERM_COLS, inverse=True).astype(BF16)
        groups = _column_groups(x_ref)
        z = jnp.concatenate(
            [jnp.dot(unperm, z[g * n:(g + 1) * n, :], preferred_element_type=F32).astype(BF16)
             for g in range(len(groups))], axis=0)
        r = jnp.dot(z, w_ref[...], preferred_element_type=F32)
        for g, idx in enumerate(groups):
            xn = residual(x_ref[idx].reshape(n, D_MODEL), r[g * n:(g + 1) * n, :])
            o_ref[idx] = xn.reshape(rows, PERM_COLS, D_MODEL)
    else:
        o_ref[...] = residual(x_ref[...], jnp.dot(z, w_ref[...], preferred_element_type=F32))


def _out_projection(y, p_gate, w_out_l, mod3, x, layer, col_major, ctx, final_g, final):
    bsz, seq, _ = x.shape
    tile, view, x_spec = _token_spec(bsz, seq, D_MODEL, col_major)
    tok = pl.BlockSpec((None, tile, D_MODEL), lambda b, t: (b, t, 0))
    out = pl.pallas_call(
        functools.partial(_outproj_kernel, final=final),
        grid=(bsz, seq // tile),
        in_specs=[tok, tok,
                  pl.BlockSpec((D_MODEL, D_MODEL), lambda b, t: (0, 0)),
                  _mod_spec(layer, 2, ctx), x_spec,
                  pl.BlockSpec((1, D_MODEL), lambda b, t: (0, 0))],
        out_specs=x_spec,
        out_shape=jax.ShapeDtypeStruct(view, F32),
        compiler_params=pltpu.CompilerParams(
            dimension_semantics=("arbitrary", "arbitrary"), vmem_limit_bytes=VMEM_LIMIT),
        name="out_projection",
    )(y, p_gate, w_out_l, mod3, x.reshape(view), final_g.reshape(1, D_MODEL))
    return out.reshape(bsz, seq, D_MODEL)


def _mixer_kernel(*refs, bcd):
    pq_ref, pab_ref, cw_ref, alog_ref, dtb_ref, ng_ref, s0f_ref, s0b_ref = refs[:8]
    n_in = 8 + (N_BCD_INPUTS if bcd else 0)
    (y_ref, sf_ref, sb_ref, qp_ref, op_ref, mp_ref, np_ref, gl_ref, o_ref, qkv_ref,
     gx_ref) = refs[n_in:n_in + 11]
    if bcd:
        bcd_stage, bcd_convs, bcd_smlp = _bcd_stages(*refs[8:n_in], y_ref, *refs[n_in + 11:])
    seq = pq_ref.shape[0]
    n_chunks = seq // GDN_CHUNK
    C = GDN_CHUNK
    W = BRANCH_W
    NG = 4 * N_HEADS

    ri = lax.broadcasted_iota(jnp.int32, (C, W), 0)
    ci = lax.broadcasted_iota(jnp.int32, (C, W), 1)
    cj = ci & (C - 1)
    low_incl = ri >= cj
    low_strict = ri > cj
    up_incl = ri <= cj
    up_strict = ri < cj
    eye_cat = (ri == cj).astype(F32)
    same16 = (ri >> 4) == (cj >> 4)
    level32 = ((ri >> 5) == (cj >> 5)) & ((ri >> 4) != (cj >> 4))
    level64 = (ri >> 5) != (cj >> 5)
    r2 = lax.broadcasted_iota(jnp.int32, (W, W), 0)
    c2 = lax.broadcasted_iota(jnp.int32, (W, W), 1)
    blockmask = (r2 >> 6) == (c2 >> 6)
    ones_blk = blockmask.astype(BF16)
    e_r = lax.broadcasted_iota(jnp.int32, (NG, 4 * W), 0)
    e_c = lax.broadcasted_iota(jnp.int32, (NG, 4 * W), 1)
    expand = (e_r == ((e_c >> 8) * N_HEADS + ((e_c & (W - 1)) >> 6))).astype(BF16)
    lane = lax.broadcasted_iota(jnp.int32, (C, AB_PAD), 1)
    rown = lax.broadcasted_iota(jnp.int32, (C, AB_PAD), 0)
    row64 = lax.broadcasted_iota(jnp.int32, (C, 3 * W), 0)

    def tile_heads(x):
        return jnp.concatenate([x] * N_HEADS, axis=0)

    def bd(x):
        return jnp.where(blockmask, tile_heads(x), 0.0)

    def fold(m):
        head = ci >> 6
        return jnp.where(head == 0, m[:C], jnp.where(head == 1, m[C:2 * C],
                                                     jnp.where(head == 2, m[2 * C:3 * C], m[3 * C:])))

    def cat_mm(a_cat, b_cat):
        return _dot(a_cat, bd(b_cat))

    def head_sums(xs):
        hi = [x.astype(BF16) for x in xs]
        lo = [(x - h.astype(F32)).astype(BF16) for x, h in zip(xs, hi)]
        r = jnp.dot(jnp.concatenate(hi + lo, axis=0), ones_blk, preferred_element_type=F32)
        n, m = len(xs), xs[0].shape[0]
        return [r[i * m:(i + 1) * m] + r[(n + i) * m:(n + i + 1) * m] for i in range(n)]

    def chunk_cumsum(x, reverse):
        s = 1
        while s < C:
            if reverse:
                x = x + jnp.where(rown < C - s, pltpu.roll(x, C - s, 0), 0.0)
            else:
                x = x + jnp.where(rown >= s, pltpu.roll(x, s, 0), 0.0)
            s *= 2
        return x

    cw = cw_ref[...]

    def conv_act(c, dep):
        r0 = pl.multiple_of(c * C, C)
        cur = pq_ref[pl.ds(r0, C), :].astype(F32) + dep
        prev = pq_ref[pl.ds(pl.multiple_of(jnp.maximum(r0 - ROW_PACK, 0), ROW_PACK), ROW_PACK), :].astype(F32)
        nxt = pq_ref[pl.ds(pl.multiple_of(jnp.minimum(r0 + C, seq - ROW_PACK), ROW_PACK), ROW_PACK), :].astype(F32)
        prev_row = prev[ROW_PACK - 1:ROW_PACK, :] * jnp.where(c > 0, 1.0, 0.0)
        next_row = nxt[0:1, :] * jnp.where(c < n_chunks - 1, 1.0, 0.0)
        up = jnp.where(row64 == 0, prev_row, pltpu.roll(cur, 1, 0))
        dn = jnp.where(row64 == C - 1, next_row, pltpu.roll(cur, C - 1, 0))
        return _silu(up * cw[0:1, :] + cur * cw[1:2, :] + dn * cw[2:3, :])

    def gate_sums(c, dep):
        ab = pab_ref[pl.ds(pl.multiple_of(c * C, C), C), :] + dep
        gk = -jnp.exp(alog_ref[...]) * _softplus(ab + dtb_ref[...])
        gb = jnp.where(lane < 2 * N_HEADS, gk, _sigmoid(ab))
        nar = jnp.where(lane < N_HEADS, chunk_cumsum(gb, False),
                        jnp.where(lane < 2 * N_HEADS, chunk_cumsum(gb, True), gb))
        return _split3(nar[:, :NG])

    G = GDN_INTERLEAVE
    n_iter = n_chunks // G

    def front_end(i, slot):
        cs = [i * G + j for j in range(G)]
        acts, nars = [], []
        pieces = [functools.partial(lambda c, dep: acts.append(conv_act(c, dep)), c) for c in cs]
        pieces += [functools.partial(lambda c, dep: nars.append(gate_sums(c, dep)), c) for c in cs]
        if bcd:
            pieces += [functools.partial(bcd_stage, c) for c in cs]

        def finish():
            sums = [head_sums([a[:, :W] * a[:, :W], a[:, W:2 * W] * a[:, W:2 * W]]) for a in acts]
            for j, (a, s, n3) in enumerate(zip(acts, sums, nars)):
                qkv_ref[slot, j, :, :W] = a[:, :W] * lax.rsqrt(s[0] + EPS) * (HEAD_DIM ** -0.5)
                qkv_ref[slot, j, :, W:2 * W] = a[:, W:2 * W] * lax.rsqrt(s[1] + EPS)
                qkv_ref[slot, j, :, 2 * W:] = a[:, 2 * W:]
                gx_ref[slot, j] = sum(jnp.dot(n, expand, preferred_element_type=F32) for n in n3)

        return pieces, finish

    def prepare(i, slot):
        pieces, finish = front_end(i, slot)
        for piece in pieces:
            piece(0.0)
        finish()

    def parallel_step(i, slot, prepare_next):
        pieces, finish = front_end(jnp.minimum(i + 1, n_iter - 1), 1 - slot) if prepare_next else ([], None)

        def fill(stage_out):
            if pieces:
                pieces.pop(0)(stage_out[0][0:1, 0:1] * 0.0)

        cs = [i * G + j for j in range(G)]
        chains = [(j, d) for j in range(G) for d in range(2)]
        q = [qkv_ref[slot, j, :, :W] for j in range(G)]
        k = [qkv_ref[slot, j, :, W:2 * W] for j in range(G)]
        v = [qkv_ref[slot, j, :, 2 * W:] for j in range(G)]
        gx = [gx_ref[slot, j] for j in range(G)]
        prod =[_dot_nt(jnp.concatenate([q[j], k[j]], axis=0), bd(k[j])) for j in range(G)]
        gi = [gx[j][:, d * W:(d + 1) * W] for j, d in chains]
        beta = [gx[j][:, (2 + d) * W:(3 + d) * W] for j, d in chains]
        incl = [up_incl if d else low_incl for _, d in chains]
        strict = [up_strict if d else low_strict for _, d in chains]
        g_end = [g[0:1, :] if d else g[C - 1:C, :] for g, (_, d) in zip(gi, chains)]
        gj = [jnp.sum(g * eye_cat, axis=0, keepdims=True) for g in gi]
        decay = [jnp.exp(jnp.where(m, g - g2, NEG_BIG)) for m, g, g2 in zip(incl, gi, gj)]
        eg = [jnp.exp(g) for g in gi]
        lmat = [-jnp.where(m, prod[j][C:] * b * dc, 0.0)
                for m, (j, _), b, dc in zip(strict, chains, beta, decay)]
        lblk = [jnp.where(same16, l, 0.0) for l in lmat]
        tinv = [eye_cat + l for l in lblk]
        power = [cat_mm(l, l) for l in lblk]
        fill(power)
        for _ in range(2):
            st = [cat_mm(jnp.concatenate([t, p], axis=0), p) for t, p in zip(tinv, power)]
            tinv = [t + s[:C] for t, s in zip(tinv, st)]
            power = [s[C:] for s in st]
            fill(st)
        st = [cat_mm(t, p) for t, p in zip(tinv, power)]
        tinv = [t + s for t, s in zip(tinv, st)]
        fill(st)
        for level in (level32, level64):
            y = [cat_mm(jnp.where(level, l, 0.0), t) for l, t in zip(lmat, tinv)]
            fill(y)
            st = [cat_mm(t, x) for t, x in zip(tinv, y)]
            tinv = [t + s for t, s in zip(tinv, st)]
            fill(st)
        tinv = [t.astype(BF16) for t in tinv]
        u = [jnp.dot(t, bd(v[j] * b).astype(BF16), preferred_element_type=F32)
             for t, (j, _), b in zip(tinv, chains, beta)]
        fill(u)
        w = [jnp.dot(t, bd(k[j] * b * e).astype(BF16), preferred_element_type=F32)
             for t, (j, _), b, e in zip(tinv, chains, beta, eg)]
        fill(w)
        attn = [jnp.where(m, prod[j][:C] * dc, 0.0).astype(BF16) for m, (j, _), dc in zip(incl, chains, decay)]
        au = [jnp.dot(a, bd(x).astype(BF16), preferred_element_type=F32) for a, x in zip(attn, u)]
        fill(au)
        aw = [jnp.dot(a, bd(x).astype(BF16), preferred_element_type=F32) for a, x in zip(attn, w)]
        fill(aw)
        kdec = [k[j] * jnp.exp(ge - g) for (j, _), ge, g in zip(chains, g_end, gi)]
        kw = [_dot_tn(kd, x) for kd, x in zip(kdec, w)]
        ku = [_dot_tn(kd, x) for kd, x in zip(kdec, u)]
        for n, (j, d) in enumerate(chains):
            rows = pl.ds(pl.multiple_of(cs[j] * C, C), C)
            qp_ref[d, rows, :] = (q[j] * eg[n] - aw[n]).astype(BF16)
            op_ref[d, rows, :] = au[n].astype(BF16)
            mp_ref[d, rows, :] = (-fold(kw[n])).astype(BF16)
            np_ref[d, rows, :] = fold(ku[n]).astype(BF16)
            gl_ref[d, pl.ds(pl.multiple_of(cs[j] * SUBLANES, SUBLANES), SUBLANES), :] = jnp.broadcast_to(
                jnp.exp(g_end[n]), (SUBLANES, W))
        if prepare_next:
            while pieces:
                fill(aw)
            finish()

    def parallel_pair(p, carry):
        parallel_step(2 * p, 0, True)
        parallel_step(2 * p + 1, 1, True)
        return carry

    prepare(0, 0)
    lax.fori_loop(0, n_iter // 2, parallel_pair, 0)
    if n_iter % 2:
        parallel_step(n_iter - 1, 0, False)

    maskb = blockmask.astype(BF16)

    def scan_step(d, c, s):
        rows = pl.ds(pl.multiple_of(c * C, C), C)
        lhs = jnp.concatenate([qp_ref[d, rows, :], tile_heads(mp_ref[d, rows, :]) * maskb], axis=0)
        t = jnp.dot(lhs, s.astype(BF16), preferred_element_type=F32)
        o = t[:C] + op_ref[d, rows, :]
        gl = gl_ref[d, pl.ds(pl.multiple_of(c * SUBLANES, SUBLANES), 1), :]
        s = s * gl + t[C:] + bd(np_ref[d, rows, :].astype(F32))
        return o, s

    def scan(i, carry, first_visit):
        s_f, s_b = carry
        o_f, s_f = scan_step(0, i, s_f)
        cb = n_chunks - 1 - i
        o_b, s_b = scan_step(1, cb, s_b)
        rows_f = pl.ds(pl.multiple_of(i * C, C), C)
        rows_b = pl.ds(pl.multiple_of(cb * C, C), C)
        if first_visit:
            o_ref[rows_f, :] = o_f
            o_ref[rows_b, :] = o_b
        else:
            o_ref[rows_f, :] += o_f
            o_ref[rows_b, :] += o_b
        if bcd:
            bcd_convs(i)
        return s_f, s_b

    half = n_chunks // 2
    states = lax.fori_loop(0, half, functools.partial(scan, first_visit=True), (s0f_ref[...], s0b_ref[...]))
    s_f, s_b = lax.fori_loop(half, n_chunks, functools.partial(scan, first_visit=False), states)
    sf_ref[...] = s_f
    sb_ref[...] = s_b

    def finish(c, carry):
        rows = pl.ds(pl.multiple_of(c * W, W), W)
        o = o_ref[rows, :]
        ms = head_sums([o * o])[0] * (1.0 / HEAD_DIM)
        y_ref[rows, :W] = (o * lax.rsqrt(ms + EPS) * ng_ref[...]).astype(y_ref.dtype)
        if bcd:
            for n in range(W // MLP_CHUNK):
                bcd_smlp(c * (W // MLP_CHUNK) + n)
        return carry

    lax.fori_loop(0, seq // W, finish, 0)


def _mixer(p_qkv, p_ab, conv_w, a_log_l, dt_bias_l, gdn_norm_g_l, s0_f, s0_b, bcd_args=None):
    bsz, seq, _ = p_qkv.shape
    W = BRANCH_W
    n_chunks = seq // GDN_CHUNK
    bcd = bcd_args is not None
    pad8 = jnp.zeros((AB_PAD - 2 * N_HEADS,), F32)
    alog_row = jnp.concatenate([a_log_l.reshape(-1), pad8]).reshape(1, AB_PAD)
    dtb_row = jnp.concatenate([dt_bias_l.reshape(-1), pad8]).reshape(1, AB_PAD)
    ng_row = jnp.tile(gdn_norm_g_l, N_HEADS).reshape(1, W)
    per_b = lambda r, wd: pl.BlockSpec((None, r, wd), lambda b: (b, 0, 0))
    const = lambda r, wd: pl.BlockSpec((r, wd), lambda b: (0, 0))
    in_specs = [per_b(seq, 3 * W), per_b(seq, AB_PAD), const(3, 3 * W), const(1, AB_PAD),
                const(1, AB_PAD), const(1, W), per_b(W, W), per_b(W, W)]
    args = [p_qkv, p_ab, conv_w, alog_row, dtb_row, ng_row, s0_f, s0_b]
    scratch = [
        pltpu.VMEM((2, seq, W), BF16),
        pltpu.VMEM((2, seq, W), BF16),
        pltpu.VMEM((2, seq, W), BF16),
        pltpu.VMEM((2, seq, W), BF16),
        pltpu.VMEM((2, n_chunks * SUBLANES, W), F32),
        pltpu.VMEM((seq, W), F32),
        pltpu.VMEM((2, GDN_INTERLEAVE, GDN_CHUNK, 3 * W), F32),
        pltpu.VMEM((2, GDN_INTERLEAVE, GDN_CHUNK, 4 * W), F32),
    ]
    if bcd:
        p_b, p_cd, short_w, conf_w, conf_b, conf_g, conf_beta, smlp_g, smlp_beta, smlp_w, smlp_b = bcd_args
        row = lambda a: a.reshape(1, W)
        ws_rows = smlp_w.reshape(N_HEADS * MLP_CHUNK, MLP_CHUNK)
        bs_exp = jnp.repeat(smlp_b.T, HEAD_DIM, axis=1)
        in_specs += [per_b(seq, 3 * W), per_b(seq, 4 * W), const(3, W), const(CONF_CONV, W)] + [const(1, W)] * 5
        in_specs += [const(N_HEADS * MLP_CHUNK, MLP_CHUNK), const(MLP_CHUNK, W)]
        args += [p_b, p_cd, short_w, conf_w, row(conf_b), row(conf_g), row(conf_beta), row(smlp_g),
                 row(smlp_beta), ws_rows, bs_exp]
        assert len(args) == 8 + N_BCD_INPUTS
        scratch += [pltpu.VMEM((seq + 2 * CONV_PAD, W), F32),
                    pltpu.VMEM((seq, W), BF16)]
    y_w = 4 * W if bcd else W
    return pl.pallas_call(
        functools.partial(_mixer_kernel, bcd=bcd),
        grid=(bsz,),
        in_specs=in_specs,
        out_specs=[per_b(seq, y_w), per_b(W, W), per_b(W, W)],
        out_shape=[jax.ShapeDtypeStruct((bsz, seq, y_w), BF16),
                   jax.ShapeDtypeStruct((bsz, W, W), F32),
                   jax.ShapeDtypeStruct((bsz, W, W), F32)],
        scratch_shapes=scratch,
        compiler_params=pltpu.CompilerParams(
            dimension_semantics=("arbitrary",), vmem_limit_bytes=VMEM_LIMIT),
        name="mixer",
    )(*args)


def _layernorm(x, g, b):
    mu = jnp.mean(x, axis=-1, keepdims=True)
    xc = x - mu
    var = jnp.mean(xc * xc, axis=-1, keepdims=True)
    return xc * lax.rsqrt(var + EPS) * g + b


N_BCD_INPUTS = 11


def _bcd_stages(pb_ref, pcd_ref, sw_ref, cw_ref, cb_ref, clg_ref, clb_ref, slg_ref, slb_ref,
                ws_ref, bs_ref, y_ref, z_ref, vln_ref):
    seq = pb_ref.shape[0]
    W = BRANCH_W
    T = GDN_CHUNK
    n_tiles = seq // T
    row = lax.broadcasted_iota(jnp.int32, (T, W), 0)
    sw = sw_ref[...]
    cw = cw_ref[...]

    z_ref[0:CONV_PAD, :] = jnp.zeros((CONV_PAD, W), F32)
    z_ref[CONV_PAD + seq:, :] = jnp.zeros((CONV_PAD, W), F32)

    def stage(t, dep):
        r0 = pl.multiple_of(t * T, T)
        cd = pcd_ref[pl.ds(r0, T), :].astype(F32) + dep
        z_ref[pl.ds(pl.multiple_of(r0 + CONV_PAD, SUBLANES), T), :] = cd[:, :W] * _sigmoid(cd[:, W:2 * W])
        vln_ref[pl.ds(r0, T), :] = _layernorm(cd[:, 3 * W:], slg_ref[...], slb_ref[...]).astype(vln_ref.dtype)

    def convs(t):
        r0 = pl.multiple_of(t * T, T)
        pb = pb_ref[pl.ds(r0, T), :].astype(F32)
        cur = pb[:, W:2 * W] * pb[:, 2 * W:]
        pv = pb_ref[pl.ds(pl.multiple_of(jnp.maximum(r0 - ROW_PACK, 0), ROW_PACK), ROW_PACK), :].astype(F32)
        nx = pb_ref[pl.ds(pl.multiple_of(jnp.minimum(r0 + T, seq - ROW_PACK), ROW_PACK), ROW_PACK), :].astype(F32)
        prev_row = (pv[:, W:2 * W] * pv[:, 2 * W:])[ROW_PACK - 1:ROW_PACK, :] * jnp.where(t > 0, 1.0, 0.0)
        next_row = (nx[:, W:2 * W] * nx[:, 2 * W:])[0:1, :] * jnp.where(t < n_tiles - 1, 1.0, 0.0)
        up = jnp.where(row == 0, prev_row, pltpu.roll(cur, 1, 0))
        dn = jnp.where(row == T - 1, next_row, pltpu.roll(cur, T - 1, 0))
        y_ref[pl.ds(r0, T), W:2 * W] = (pb[:, :W] * (up * sw[0:1, :] + cur * sw[1:2, :] + dn * sw[2:3, :])
                                        ).astype(y_ref.dtype)
        acc = jnp.zeros((T, W), F32)
        for r in range(SUBLANES):
            part = None
            for a in range(4):
                o = SUBLANES * a + r
                if o < 1 or o > CONF_CONV:
                    continue
                win = z_ref[pl.ds(pl.multiple_of(r0 + SUBLANES * a, SUBLANES), T + SUBLANES), :]
                term = win * cw[o - 1:o, :]
                part = term if part is None else part + term
            if r:
                part = pltpu.roll(part, T + SUBLANES - r, 0)
            acc = acc + part[:T, :]
        zc = _layernorm(acc + cb_ref[...], clg_ref[...], clb_ref[...])
        y_ref[pl.ds(r0, T), 2 * W:3 * W] = _silu(zc).astype(y_ref.dtype)

    lane_grp = lax.broadcasted_iota(jnp.int32, (MLP_CHUNK, W), 1) >> 6
    wsb = ws_ref[...].astype(BF16)

    def smlp(n):
        r0 = pl.multiple_of(n * MLP_CHUNK, MLP_CHUNK)
        full = jnp.dot(wsb, vln_ref[pl.ds(r0, MLP_CHUNK), :].astype(BF16), preferred_element_type=F32)
        mixed = bs_ref[...]
        for g in range(N_HEADS):
            mixed = mixed + jnp.where(lane_grp == g, full[g * MLP_CHUNK:(g + 1) * MLP_CHUNK, :], 0.0)
        u = pcd_ref[pl.ds(r0, MLP_CHUNK), 2 * W:3 * W].astype(F32)
        y_ref[pl.ds(r0, MLP_CHUNK), 3 * W:] = (u * mixed).astype(y_ref.dtype)

    return stage, convs, smlp


def _permute_w_in(w_in):
    depth = w_in.shape[0]
    b0 = A_COLS
    c0 = b0 + B_COLS
    g0 = c0 + C_COLS + D_COLS
    w_in = w_in.astype(BF16)
    pad = jnp.zeros((depth, D_MODEL, AB_PAD - 4 * N_HEADS), BF16)
    return jnp.concatenate(
        [w_in[:, :, :3 * BRANCH_W], w_in[:, :, b0:c0], w_in[:, :, c0:g0], w_in[:, :, g0:],
         w_in[:, :, 3 * BRANCH_W:A_COLS], pad], axis=-1)


FULL_WIDTHS = (3 * BRANCH_W, 3 * BRANCH_W, 4 * BRANCH_W, D_MODEL, AB_PAD)


def kernel(x, c, ctx, c_ctx, norm_g, w_ada, b_ada, w_in, qkv_conv_w, a_log, dt_bias, gdn_norm_g,
           short_conv_w, conf_conv_w, conf_conv_b, conf_ln_g, conf_ln_b, smlp_ln_g, smlp_ln_b,
           smlp_w, smlp_b, w_out, final_g):
    bsz, seq, _ = x.shape
    depth = w_in.shape[0]
    assert bsz + 1 <= MOD_ROWS - 7 and seq % TOKEN_TILE == 0

    cvec = jnp.zeros((MOD_ROWS, D_MODEL), F32).at[:bsz].set(c).at[MOD_ROWS - 8].set(c_ctx)
    mod3 = _modulation(cvec, w_ada, b_ada).reshape(depth * MOD_ROWS, 1, 3 * D_MODEL)
    w_perm = _permute_w_in(w_in)
    w_out_b = w_out.astype(BF16)
    s_zero = jnp.zeros((bsz, BRANCH_W, BRANCH_W), F32)
    qkv_only = 3 * BRANCH_W

    xc = ctx
    for l in range(depth):
        last = l == depth - 1
        col_major = l % 2 == 1
        mix_args = (short_conv_w[l], conf_conv_w[l], conf_conv_b[l], conf_ln_g[l], conf_ln_b[l],
                    smlp_ln_g[l], smlp_ln_b[l], smlp_w[l], smlp_b[l])
        if last:
            w_ctx = jnp.concatenate([w_perm[l][:, :qkv_only], w_perm[l][:, -AB_PAD:]], axis=-1)
            pc_qkv, pc_ab = _in_projection(xc, mod3, norm_g[l], w_ctx, l, (qkv_only, AB_PAD), False, True)
        else:
            pc_qkv, pc_b, pc_cd, pc_gate, pc_ab = _in_projection(
                xc, mod3, norm_g[l], w_perm[l], l, FULL_WIDTHS, False, True)
        gdn_args = (qkv_conv_w[l], a_log[l], dt_bias[l], gdn_norm_g[l])
        yc, s_f, s_b = _mixer(pc_qkv, pc_ab, *gdn_args, s_zero, s_zero,
                              None if last else (pc_b, pc_cd) + mix_args)
        p_qkv, p_b, p_cd, p_gate, p_ab = _in_projection(
            x, mod3, norm_g[l], w_perm[l], l, FULL_WIDTHS, col_major, False)
        y, _, _ = _mixer(p_qkv, p_ab, *gdn_args, s_f, s_b, (p_b, p_cd) + mix_args)
        x = _out_projection(y, p_gate, w_out_b[l], mod3, x, l, col_major, False, final_g, last)
        if not last:
            xc = _out_projection(yc, pc_gate, w_out_b[l], mod3, xc, l, False, True, final_g, False)
    return x
```

```python
import functools

import jax
import jax.numpy as jnp
from jax import lax
from jax.experimental import pallas as pl
from jax.experimental.pallas import tpu as pltpu

F32 = jnp.float32
BF16 = jnp.bfloat16

D_MODEL = 1024
DEPTH = 4
GRID_W = 64
HEAD_DIM = 64
BRANCH_W = 256
N_HEADS = 4
A_COLS = 3 * BRANCH_W + 4 * N_HEADS
B_COLS = 3 * BRANCH_W
C_COLS = 2 * BRANCH_W
D_COLS = 2 * BRANCH_W
IN_COLS = A_COLS + B_COLS + C_COLS + D_COLS + D_MODEL
CONF_CONV = 31
GDN_CHUNK = 64
MLP_CHUNK = 128
EPS = 1e-6

LANES = 128
SUBLANES = 8
ROW_PACK = 16
AB_PAD = LANES
MOD_ROWS = 24
TOKEN_TILE = 1024
PERM_COLS = 8
CONV_PAD = 16
VMEM_LIMIT = 58 * 1024 * 1024
NEG_BIG = -1e30
GDN_INTERLEAVE = 4


def _sigmoid(x):
    return 1.0 / (1.0 + jnp.exp(-x))


def _silu(x):
    return x * _sigmoid(x)


def _softplus(x):
    return jnp.maximum(x, 0.0) + jnp.log1p(jnp.exp(-jnp.abs(x)))


def _dot(a, b):
    return jnp.dot(a.astype(BF16), b.astype(BF16), preferred_element_type=F32)


def _dot_nt(a, b):
    return lax.dot_general(a.astype(BF16), b.astype(BF16), (((1,), (1,)), ((), ())),
                           preferred_element_type=F32)


def _dot_tn(a, b):
    return lax.dot_general(a.astype(BF16), b.astype(BF16), (((0,), (0,)), ((), ())),
                           preferred_element_type=F32)


def _split3(x):
    x1 = x.astype(BF16)
    r1 = x - x1.astype(F32)
    x2 = r1.astype(BF16)
    x3 = (r1 - x2.astype(F32)).astype(BF16)
    return x1, x2, x3


def _mod_kernel(c_ref, w_ref, b_ref, o_ref):
    s = _silu(c_ref[...])
    o_ref[...] = jnp.dot(s, w_ref[...], preferred_element_type=F32) + b_ref[...]


def _modulation(cvec, w_ada, b_ada):
    depth = w_ada.shape[0]
    return pl.pallas_call(
        _mod_kernel,
        grid=(depth, 3),
        in_specs=[
            pl.BlockSpec((MOD_ROWS, D_MODEL), lambda l, k: (0, 0)),
            pl.BlockSpec((None, D_MODEL, D_MODEL), lambda l, k: (l, 0, k)),
            pl.BlockSpec((None, 1, D_MODEL), lambda l, k: (l, 0, k)),
        ],
        out_specs=pl.BlockSpec((None, MOD_ROWS, D_MODEL), lambda l, k: (l, 0, k)),
        out_shape=jax.ShapeDtypeStruct((depth, MOD_ROWS, 3 * D_MODEL), F32),
        compiler_params=pltpu.CompilerParams(
            dimension_semantics=("arbitrary", "arbitrary"), vmem_limit_bytes=VMEM_LIMIT),
        name="modulation",
    )(cvec, w_ada, b_ada.reshape(depth, 1, 3 * D_MODEL))


def _to_sequence_order(n_rows, n_cols, inverse=False):
    i = lax.broadcasted_iota(jnp.int32, (n_rows * n_cols, n_rows * n_cols), 0)
    j = lax.broadcasted_iota(jnp.int32, (n_rows * n_cols, n_rows * n_cols), 1)
    if inverse:
        i, j = j, i
    return j == (i % n_rows) * n_cols + i // n_rows


def _column_groups(ref):
    return [(slice(None), slice(g * PERM_COLS, (g + 1) * PERM_COLS), slice(None))
            for g in range(ref.shape[1] // PERM_COLS)]


def _inproj_kernel(x_ref, sh_ref, sc_ref, g_ref, w_ref, *out_refs, widths):
    def modulated(x):
        ms = jnp.mean(x * x, axis=-1, keepdims=True)
        h = x * lax.rsqrt(ms + EPS) * g_ref[...]
        return (h * (1.0 + sc_ref[...]) + sh_ref[...]).astype(BF16)

    if len(x_ref.shape) == 3:
        rows = x_ref.shape[0]
        perm = _to_sequence_order(rows, PERM_COLS).astype(BF16)
        hb = jnp.concatenate(
            [jnp.dot(perm, modulated(x_ref[idx].reshape(rows * PERM_COLS, D_MODEL)),
                     preferred_element_type=F32).astype(BF16) for idx in _column_groups(x_ref)], axis=0)
    else:
        hb = modulated(x_ref[...])
    off = 0
    for o_ref, wd in zip(out_refs, widths):
        o_ref[...] = jnp.dot(hb, w_ref[:, off:off + wd], preferred_element_type=F32).astype(o_ref.dtype)
        off += wd


def _mod_spec(layer, kind, ctx):
    if ctx:
        return pl.BlockSpec((None, 1, D_MODEL), lambda b, t: (layer * MOD_ROWS + MOD_ROWS - 8, 0, kind))
    return pl.BlockSpec((None, 1, D_MODEL), lambda b, t: (layer * MOD_ROWS + b, 0, kind))


def _token_spec(bsz, seq, width, col_major):
    tile = min(TOKEN_TILE, seq)
    if not col_major:
        return tile, (bsz, seq, width), pl.BlockSpec((None, tile, width), lambda b, t: (b, t, 0))
    rows = seq // GRID_W
    assert tile % (rows * PERM_COLS) == 0
    return (tile, (bsz, rows, GRID_W, width),
            pl.BlockSpec((None, rows, tile // rows, width), lambda b, t: (b, 0, t, 0)))


def _in_projection(x, mod3, norm_g_l, w_l, layer, widths, col_major, ctx):
    bsz, seq, _ = x.shape
    tile, view, x_spec = _token_spec(bsz, seq, D_MODEL, col_major)
    n_cols = sum(widths)
    kern = functools.partial(_inproj_kernel, widths=tuple(widths))
    return pl.pallas_call(
        kern,
        grid=(bsz, seq // tile),
        in_specs=[x_spec] + [
            _mod_spec(layer, 0, ctx), _mod_spec(layer, 1, ctx),
            pl.BlockSpec((1, D_MODEL), lambda b, t: (0, 0)),
            pl.BlockSpec((D_MODEL, n_cols), lambda b, t: (0, 0)),
        ],
        out_specs=[pl.BlockSpec((None, tile, wd), lambda b, t: (b, t, 0)) for wd in widths],
        out_shape=[jax.ShapeDtypeStruct((bsz, seq, wd), F32 if wd == AB_PAD else BF16) for wd in widths],
        compiler_params=pltpu.CompilerParams(
            dimension_semantics=("arbitrary", "arbitrary"), vmem_limit_bytes=VMEM_LIMIT),
        name="in_projection",
    )(x.reshape(view), mod3, mod3, norm_g_l.reshape(1, D_MODEL), w_l)


def _outproj_kernel(y_ref, pg_ref, w_ref, gate_ref, x_ref, fg_ref, o_ref, *, final):
    def residual(x, r):
        xn = x + gate_ref[...] * r
        if final:
            ms = jnp.mean(xn * xn, axis=-1, keepdims=True)
            xn = xn * lax.rsqrt(ms + EPS) * fg_ref[...]
        return xn

    z = (y_ref[...].astype(F32) * _silu(pg_ref[...].astype(F32))).astype(BF16)
    if len(x_ref.shape) == 3:
        rows = x_ref.shape[0]
        n = rows * PERM_COLS
        unperm = _to_sequence_order(rows, PERM_COLS, inverse=True).astype(BF16)
        groups = _column_groups(x_ref)
        z = jnp.concatenate(
            [jnp.dot(unperm, z[g * n:(g + 1) * n, :], preferred_element_type=F32).astype(BF16)
             for g in range(len(groups))], axis=0)
        r = jnp.dot(z, w_ref[...], preferred_element_type=F32)
        for g, idx in enumerate(groups):
            xn = residual(x_ref[idx].reshape(n, D_MODEL), r[g * n:(g + 1) * n, :])
            o_ref[idx] = xn.reshape(rows, PERM_COLS, D_MODEL)
    else:
        o_ref[...] = residual(x_ref[...], jnp.dot(z, w_ref[...], preferred_element_type=F32))


def _out_projection(y, p_gate, w_out_l, mod3, x, layer, col_major, ctx, final_g, final):
    bsz, seq, _ = x.shape
    tile, view, x_spec = _token_spec(bsz, seq, D_MODEL, col_major)
    tok = pl.BlockSpec((None, tile, D_MODEL), lambda b, t: (b, t, 0))
    out = pl.pallas_call(
        functools.partial(_outproj_kernel, final=final),
        grid=(bsz, seq // tile),
        in_specs=[tok, tok,
                  pl.BlockSpec((D_MODEL, D_MODEL), lambda b, t: (0, 0)),
                  _mod_spec(layer, 2, ctx), x_spec,
                  pl.BlockSpec((1, D_MODEL), lambda b, t: (0, 0))],
        out_specs=x_spec,
        out_shape=jax.ShapeDtypeStruct(view, F32),
        compiler_params=pltpu.CompilerParams(
            dimension_semantics=("arbitrary", "arbitrary"), vmem_limit_bytes=VMEM_LIMIT),
        name="out_projection",
    )(y, p_gate, w_out_l, mod3, x.reshape(view), final_g.reshape(1, D_MODEL))
    return out.reshape(bsz, seq, D_MODEL)


def _mixer_kernel(*refs, bcd):
    pq_ref, pab_ref, cw_ref, alog_ref, dtb_ref, ng_ref, s0f_ref, s0b_ref = refs[:8]
    n_in = 8 + (N_BCD_INPUTS if bcd else 0)
    (y_ref, sf_ref, sb_ref, u_ref, w_ref, at_ref, kd_ref, qe_ref, gl_ref, o_ref, qkv_ref,
     gx_ref) = refs[n_in:n_in + 12]
    if bcd:
        bcd_stage, bcd_convs, bcd_smlp = _bcd_stages(*refs[8:n_in], y_ref, *refs[n_in + 12:])
    seq = pq_ref.shape[0]
    n_chunks = seq // GDN_CHUNK
    C = GDN_CHUNK
    W = BRANCH_W
    NG = 4 * N_HEADS

    ri = lax.broadcasted_iota(jnp.int32, (C, W), 0)
    ci = lax.broadcasted_iota(jnp.int32, (C, W), 1)
    cj = ci & (C - 1)
    low_incl = ri >= cj
    low_strict = ri > cj
    up_incl = ri <= cj
    up_strict = ri < cj
    eye_cat = (ri == cj).astype(F32)
    same16 = (ri >> 4) == (cj >> 4)
    level32 = ((ri >> 5) == (cj >> 5)) & ((ri >> 4) != (cj >> 4))
    level64 = (ri >> 5) != (cj >> 5)
    r2 = lax.broadcasted_iota(jnp.int32, (W, W), 0)
    c2 = lax.broadcasted_iota(jnp.int32, (W, W), 1)
    blockmask = (r2 >> 6) == (c2 >> 6)
    ones_blk = blockmask.astype(BF16)
    e_r = lax.broadcasted_iota(jnp.int32, (NG, 4 * W), 0)
    e_c = lax.broadcasted_iota(jnp.int32, (NG, 4 * W), 1)
    expand = (e_r == ((e_c >> 8) * N_HEADS + ((e_c & (W - 1)) >> 6))).astype(BF16)
    lane = lax.broadcasted_iota(jnp.int32, (C, AB_PAD), 1)
    rown = lax.broadcasted_iota(jnp.int32, (C, AB_PAD), 0)
    row64 = lax.broadcasted_iota(jnp.int32, (C, 3 * W), 0)

    def tile_heads(x):
        return jnp.concatenate([x] * N_HEADS, axis=0)

    def bd(x):
        return jnp.where(blockmask, tile_heads(x), 0.0)

    def cat_mm(a_cat, b_cat):
        return _dot(a_cat, bd(b_cat))

    def head_sums(xs):
        hi = [x.astype(BF16) for x in xs]
        lo = [(x - h.astype(F32)).astype(BF16) for x, h in zip(xs, hi)]
        r = jnp.dot(jnp.concatenate(hi + lo, axis=0), ones_blk, preferred_element_type=F32)
        n, m = len(xs), xs[0].shape[0]
        return [r[i * m:(i + 1) * m] + r[(n + i) * m:(n + i + 1) * m] for i in range(n)]

    def chunk_cumsum(x, reverse):
        s = 1
        while s < C:
            if reverse:
                x = x + jnp.where(rown < C - s, pltpu.roll(x, C - s, 0), 0.0)
            else:
                x = x + jnp.where(rown >= s, pltpu.roll(x, s, 0), 0.0)
            s *= 2
        return x

    cw = cw_ref[...]

    def conv_act(c, dep):
        r0 = pl.multiple_of(c * C, C)
        cur = pq_ref[pl.ds(r0, C), :].astype(F32) + dep
        prev = pq_ref[pl.ds(pl.multiple_of(jnp.maximum(r0 - ROW_PACK, 0), ROW_PACK), ROW_PACK), :].astype(F32)
        nxt = pq_ref[pl.ds(pl.multiple_of(jnp.minimum(r0 + C, seq - ROW_PACK), ROW_PACK), ROW_PACK), :].astype(F32)
        prev_row = prev[ROW_PACK - 1:ROW_PACK, :] * jnp.where(c > 0, 1.0, 0.0)
        next_row = nxt[0:1, :] * jnp.where(c < n_chunks - 1, 1.0, 0.0)
        up = jnp.where(row64 == 0, prev_row, pltpu.roll(cur, 1, 0))
        dn = jnp.where(row64 == C - 1, next_row, pltpu.roll(cur, C - 1, 0))
        return _silu(up * cw[0:1, :] + cur * cw[1:2, :] + dn * cw[2:3, :])

    def gate_sums(c, dep):
        ab = pab_ref[pl.ds(pl.multiple_of(c * C, C), C), :] + dep
        gk = -jnp.exp(alog_ref[...]) * _softplus(ab + dtb_ref[...])
        gb = jnp.where(lane < 2 * N_HEADS, gk, _sigmoid(ab))
        nar = jnp.where(lane < N_HEADS, chunk_cumsum(gb, False),
                        jnp.where(lane < 2 * N_HEADS, chunk_cumsum(gb, True), gb))
        return _split3(nar[:, :NG])

    G = qkv_ref.shape[1]
    n_iter = n_chunks // G

    def front_end(i, slot):
        cs = [i * G + j for j in range(G)]
        acts, nars = [], []
        pieces = [functools.partial(lambda c, dep: acts.append(conv_act(c, dep)), c) for c in cs]
        pieces += [functools.partial(lambda c, dep: nars.append(gate_sums(c, dep)), c) for c in cs]
        if bcd:
            pieces += [functools.partial(bcd_stage, c) for c in cs]

        def finish():
            sums = [head_sums([a[:, :W] * a[:, :W], a[:, W:2 * W] * a[:, W:2 * W]]) for a in acts]
            for j, (a, s, n3) in enumerate(zip(acts, sums, nars)):
                qkv_ref[slot, j, :, :W] = a[:, :W] * lax.rsqrt(s[0] + EPS) * (HEAD_DIM ** -0.5)
                qkv_ref[slot, j, :, W:2 * W] = a[:, W:2 * W] * lax.rsqrt(s[1] + EPS)
                qkv_ref[slot, j, :, 2 * W:] = a[:, 2 * W:]
                gx_ref[slot, j] = sum(jnp.dot(n, expand, preferred_element_type=F32) for n in n3)

        return pieces, finish

    def prepare(i, slot):
        pieces, finish = front_end(i, slot)
        for piece in pieces:
            piece(0.0)
        finish()

    def parallel_step(i, slot, prepare_next):
        pieces, finish = front_end(jnp.minimum(i + 1, n_iter - 1), 1 - slot) if prepare_next else ([], None)

        def fill(stage_out):
            if pieces:
                pieces.pop(0)(stage_out[0][0:1, 0:1] * 0.0)

        cs = [i * G + j for j in range(G)]
        chains = [(j, d) for j in range(G) for d in range(2)]
        q = [qkv_ref[slot, j, :, :W] for j in range(G)]
        k = [qkv_ref[slot, j, :, W:2 * W] for j in range(G)]
        v = [qkv_ref[slot, j, :, 2 * W:] for j in range(G)]
        gx = [gx_ref[slot, j] for j in range(G)]
        prod =[_dot_nt(jnp.concatenate([q[j], k[j]], axis=0), bd(k[j])) for j in range(G)]
        gi = [gx[j][:, d * W:(d + 1) * W] for j, d in chains]
        beta = [gx[j][:, (2 + d) * W:(3 + d) * W] for j, d in chains]
        incl = [up_incl if d else low_incl for _, d in chains]
        strict = [up_strict if d else low_strict for _, d in chains]
        g_end = [g[0:1, :] if d else g[C - 1:C, :] for g, (_, d) in zip(gi, chains)]
        gj = [jnp.sum(g * eye_cat, axis=0, keepdims=True) for g in gi]
        decay = [jnp.exp(jnp.where(m, g - g2, NEG_BIG)) for m, g, g2 in zip(incl, gi, gj)]
        eg = [jnp.exp(g) for g in gi]
        lmat = [-jnp.where(m, prod[j][C:] * b * dc, 0.0)
                for m, (j, _), b, dc in zip(strict, chains, beta, decay)]
        lblk = [jnp.where(same16, l, 0.0) for l in lmat]
        tinv = [eye_cat + l for l in lblk]
        power = [cat_mm(l, l) for l in lblk]
        fill(power)
        for _ in range(2):
            st = [cat_mm(jnp.concatenate([t, p], axis=0), p) for t, p in zip(tinv, power)]
            tinv = [t + s[:C] for t, s in zip(tinv, st)]
            power = [s[C:] for s in st]
            fill(st)
        st = [cat_mm(t, p) for t, p in zip(tinv, power)]
        tinv = [t + s for t, s in zip(tinv, st)]
        fill(st)
        for level in (level32, level64):
            y = [cat_mm(jnp.where(level, l, 0.0), t) for l, t in zip(lmat, tinv)]
            fill(y)
            st = [cat_mm(t, x) for t, x in zip(tinv, y)]
            tinv = [t + s for t, s in zip(tinv, st)]
            fill(st)
        tinv = [t.astype(BF16) for t in tinv]
        u = [jnp.dot(t, bd(v[j] * b).astype(BF16), preferred_element_type=F32)
             for t, (j, _), b in zip(tinv, chains, beta)]
        fill(u)
        w = [jnp.dot(t, bd(k[j] * b * e).astype(BF16), preferred_element_type=F32)
             for t, (j, _), b, e in zip(tinv, chains, beta, eg)]
        fill(w)
        for n, (j, d) in enumerate(chains):
            rows = pl.ds(pl.multiple_of(cs[j] * C, C), C)
            u_ref[d, rows, :] = u[n].astype(BF16)
            w_ref[d, rows, :] = (-w[n]).astype(BF16)
            at_ref[d, rows, :] = jnp.where(incl[n], prod[j][:C] * decay[n], 0.0).astype(BF16)
            kd_ref[d, rows, :] = (k[j] * jnp.exp(g_end[n] - gi[n])).astype(BF16)
            qe_ref[d, rows, :] = (q[j] * eg[n]).astype(BF16)
            gl_ref[d, pl.ds(pl.multiple_of(cs[j] * SUBLANES, SUBLANES), SUBLANES), :] = jnp.broadcast_to(
                jnp.exp(g_end[n]), (SUBLANES, W))
        if prepare_next:
            while pieces:
                fill(w)
            finish()

    def parallel_pair(p, carry):
        parallel_step(2 * p, 0, True)
        parallel_step(2 * p + 1, 1, True)
        return carry

    prepare(0, 0)
    lax.fori_loop(0, n_iter // 2, parallel_pair, 0)
    if n_iter % 2:
        parallel_step(n_iter - 1, 0, False)

    maskb = blockmask.astype(BF16)

    def state_free(d, c):
        rows = pl.ds(pl.multiple_of(c * C, C), C)
        u, w, at, kd = u_ref[d, rows, :], w_ref[d, rows, :], at_ref[d, rows, :], kd_ref[d, rows, :]
        au = jnp.dot(at, tile_heads(u) * maskb, preferred_element_type=F32)
        aw = jnp.dot(at, tile_heads(w) * maskb, preferred_element_type=F32)
        n_all = _dot_tn(kd, u)
        m_all = _dot_tn(kd, w)
        lhs = jnp.concatenate([(qe_ref[d, rows, :].astype(F32) + aw).astype(BF16), m_all.astype(BF16)], axis=0)
        gl = gl_ref[d, pl.ds(pl.multiple_of(c * SUBLANES, SUBLANES), 1), :]
        return lhs, au, n_all, gl

    def state_step(parts, s):
        lhs, au, n_all, gl = parts
        t = jnp.dot(lhs, s.astype(BF16), preferred_element_type=F32)
        return t[:C] + au, s * gl + jnp.where(blockmask, t[C:] + n_all, 0.0)

    def scan(i, carry, first_visit):
        s_f, s_b = carry
        cb = n_chunks - 1 - i
        parts_f, parts_b = state_free(0, i), state_free(1, cb)
        o_f, s_f = state_step(parts_f, s_f)
        o_b, s_b = state_step(parts_b, s_b)
        rows_f = pl.ds(pl.multiple_of(i * C, C), C)
        rows_b = pl.ds(pl.multiple_of(cb * C, C), C)
        if first_visit:
            o_ref[rows_f, :] = o_f
            o_ref[rows_b, :] = o_b
        else:
            o_ref[rows_f, :] += o_f
            o_ref[rows_b, :] += o_b
        if bcd:
            bcd_convs(i)
        return s_f, s_b

    half = n_chunks // 2
    states = lax.fori_loop(0, half, functools.partial(scan, first_visit=True), (s0f_ref[...], s0b_ref[...]))
    s_f, s_b = lax.fori_loop(half, n_chunks, functools.partial(scan, first_visit=False), states)
    sf_ref[...] = s_f
    sb_ref[...] = s_b

    def finish(c, carry):
        rows = pl.ds(pl.multiple_of(c * W, W), W)
        o = o_ref[rows, :]
        ms = head_sums([o * o])[0] * (1.0 / HEAD_DIM)
        y_ref[rows, :W] = (o * lax.rsqrt(ms + EPS) * ng_ref[...]).astype(y_ref.dtype)
        if bcd:
            for n in range(W // MLP_CHUNK):
                bcd_smlp(c * (W // MLP_CHUNK) + n)
        return carry

    lax.fori_loop(0, seq // W, finish, 0)


def _mixer(p_qkv, p_ab, conv_w, a_log_l, dt_bias_l, gdn_norm_g_l, s0_f, s0_b, bcd_args=None):
    bsz, seq, _ = p_qkv.shape
    W = BRANCH_W
    n_chunks = seq // GDN_CHUNK
    interleave = min(GDN_INTERLEAVE, n_chunks)
    bcd = bcd_args is not None
    pad8 = jnp.zeros((AB_PAD - 2 * N_HEADS,), F32)
    alog_row = jnp.concatenate([a_log_l.reshape(-1), pad8]).reshape(1, AB_PAD)
    dtb_row = jnp.concatenate([dt_bias_l.reshape(-1), pad8]).reshape(1, AB_PAD)
    ng_row = jnp.tile(gdn_norm_g_l, N_HEADS).reshape(1, W)
    per_b = lambda r, wd: pl.BlockSpec((None, r, wd), lambda b: (b, 0, 0))
    const = lambda r, wd: pl.BlockSpec((r, wd), lambda b: (0, 0))
    in_specs = [per_b(seq, 3 * W), per_b(seq, AB_PAD), const(3, 3 * W), const(1, AB_PAD),
                const(1, AB_PAD), const(1, W), per_b(W, W), per_b(W, W)]
    args = [p_qkv, p_ab, conv_w, alog_row, dtb_row, ng_row, s0_f, s0_b]
    scratch = [
        pltpu.VMEM((2, seq, W), BF16),
        pltpu.VMEM((2, seq, W), BF16),
        pltpu.VMEM((2, seq, W), BF16),
        pltpu.VMEM((2, seq, W), BF16),
        pltpu.VMEM((2, seq, W), BF16),
        pltpu.VMEM((2, n_chunks * SUBLANES, W), F32),
        pltpu.VMEM((seq, W), F32),
        pltpu.VMEM((2, interleave, GDN_CHUNK, 3 * W), F32),
        pltpu.VMEM((2, interleave, GDN_CHUNK, 4 * W), F32),
    ]
    if bcd:
        p_b, p_cd, short_w, conf_w, conf_b, conf_g, conf_beta, smlp_g, smlp_beta, smlp_w, smlp_b = bcd_args
        row = lambda a: a.reshape(1, W)
        ws_rows = smlp_w.reshape(N_HEADS * MLP_CHUNK, MLP_CHUNK)
        bs_exp = jnp.repeat(smlp_b.T, HEAD_DIM, axis=1)
        in_specs += [per_b(seq, 3 * W), per_b(seq, 4 * W), const(3, W), const(CONF_CONV, W)] + [const(1, W)] * 5
        in_specs += [const(N_HEADS * MLP_CHUNK, MLP_CHUNK), const(MLP_CHUNK, W)]
        args += [p_b, p_cd, short_w, conf_w, row(conf_b), row(conf_g), row(conf_beta), row(smlp_g),
                 row(smlp_beta), ws_rows, bs_exp]
        assert len(args) == 8 + N_BCD_INPUTS
        scratch += [pltpu.VMEM((seq + 2 * CONV_PAD, W), F32),
                    pltpu.VMEM((seq, W), BF16)]
    y_w = 4 * W if bcd else W
    return pl.pallas_call(
        functools.partial(_mixer_kernel, bcd=bcd),
        grid=(bsz,),
        in_specs=in_specs,
        out_specs=[per_b(seq, y_w), per_b(W, W), per_b(W, W)],
        out_shape=[jax.ShapeDtypeStruct((bsz, seq, y_w), BF16),
                   jax.ShapeDtypeStruct((bsz, W, W), F32),
                   jax.ShapeDtypeStruct((bsz, W, W), F32)],
        scratch_shapes=scratch,
        compiler_params=pltpu.CompilerParams(
            dimension_semantics=("arbitrary",), vmem_limit_bytes=VMEM_LIMIT),
        name="mixer",
    )(*args)


def _layernorm(x, g, b):
    mu = jnp.mean(x, axis=-1, keepdims=True)
    xc = x - mu
    var = jnp.mean(xc * xc, axis=-1, keepdims=True)
    return xc * lax.rsqrt(var + EPS) * g + b


N_BCD_INPUTS = 11


def _bcd_stages(pb_ref, pcd_ref, sw_ref, cw_ref, cb_ref, clg_ref, clb_ref, slg_ref, slb_ref,
                ws_ref, bs_ref, y_ref, z_ref, vln_ref):
    seq = pb_ref.shape[0]
    W = BRANCH_W
    T = GDN_CHUNK
    n_tiles = seq // T
    row = lax.broadcasted_iota(jnp.int32, (T, W), 0)
    sw = sw_ref[...]
    cw = cw_ref[...]

    z_ref[0:CONV_PAD, :] = jnp.zeros((CONV_PAD, W), F32)
    z_ref[CONV_PAD + seq:, :] = jnp.zeros((CONV_PAD, W), F32)

    def stage(t, dep):
        r0 = pl.multiple_of(t * T, T)
        cd = pcd_ref[pl.ds(r0, T), :].astype(F32) + dep
        z_ref[pl.ds(pl.multiple_of(r0 + CONV_PAD, SUBLANES), T), :] = cd[:, :W] * _sigmoid(cd[:, W:2 * W])
        vln_ref[pl.ds(r0, T), :] = _layernorm(cd[:, 3 * W:], slg_ref[...], slb_ref[...]).astype(vln_ref.dtype)

    def convs(t):
        r0 = pl.multiple_of(t * T, T)
        pb = pb_ref[pl.ds(r0, T), :].astype(F32)
        cur = pb[:, W:2 * W] * pb[:, 2 * W:]
        pv = pb_ref[pl.ds(pl.multiple_of(jnp.maximum(r0 - ROW_PACK, 0), ROW_PACK), ROW_PACK), :].astype(F32)
        nx = pb_ref[pl.ds(pl.multiple_of(jnp.minimum(r0 + T, seq - ROW_PACK), ROW_PACK), ROW_PACK), :].astype(F32)
        prev_row = (pv[:, W:2 * W] * pv[:, 2 * W:])[ROW_PACK - 1:ROW_PACK, :] * jnp.where(t > 0, 1.0, 0.0)
        next_row = (nx[:, W:2 * W] * nx[:, 2 * W:])[0:1, :] * jnp.where(t < n_tiles - 1, 1.0, 0.0)
        up = jnp.where(row == 0, prev_row, pltpu.roll(cur, 1, 0))
        dn = jnp.where(row == T - 1, next_row, pltpu.roll(cur, T - 1, 0))
        y_ref[pl.ds(r0, T), W:2 * W] = (pb[:, :W] * (up * sw[0:1, :] + cur * sw[1:2, :] + dn * sw[2:3, :])
                                        ).astype(y_ref.dtype)
        acc = jnp.zeros((T, W), F32)
        for r in range(SUBLANES):
            part = None
            for a in range(4):
                o = SUBLANES * a + r
                if o < 1 or o > CONF_CONV:
                    continue
                win = z_ref[pl.ds(pl.multiple_of(r0 + SUBLANES * a, SUBLANES), T + SUBLANES), :]
                term = win * cw[o - 1:o, :]
                part = term if part is None else part + term
            if r:
                part = pltpu.roll(part, T + SUBLANES - r, 0)
            acc = acc + part[:T, :]
        zc = _layernorm(acc + cb_ref[...], clg_ref[...], clb_ref[...])
        y_ref[pl.ds(r0, T), 2 * W:3 * W] = _silu(zc).astype(y_ref.dtype)

    lane_grp = lax.broadcasted_iota(jnp.int32, (MLP_CHUNK, W), 1) >> 6
    wsb = ws_ref[...].astype(BF16)

    def smlp(n):
        r0 = pl.multiple_of(n * MLP_CHUNK, MLP_CHUNK)
        full = jnp.dot(wsb, vln_ref[pl.ds(r0, MLP_CHUNK), :].astype(BF16), preferred_element_type=F32)
        mixed = bs_ref[...]
        for g in range(N_HEADS):
            mixed = mixed + jnp.where(lane_grp == g, full[g * MLP_CHUNK:(g + 1) * MLP_CHUNK, :], 0.0)
        u = pcd_ref[pl.ds(r0, MLP_CHUNK), 2 * W:3 * W].astype(F32)
        y_ref[pl.ds(r0, MLP_CHUNK), 3 * W:] = (u * mixed).astype(y_ref.dtype)

    return stage, convs, smlp


def _permute_w_in(w_in):
    depth = w_in.shape[0]
    b0 = A_COLS
    c0 = b0 + B_COLS
    g0 = c0 + C_COLS + D_COLS
    w_in = w_in.astype(BF16)
    pad = jnp.zeros((depth, D_MODEL, AB_PAD - 4 * N_HEADS), BF16)
    return jnp.concatenate(
        [w_in[:, :, :3 * BRANCH_W], w_in[:, :, b0:c0], w_in[:, :, c0:g0], w_in[:, :, g0:],
         w_in[:, :, 3 * BRANCH_W:A_COLS], pad], axis=-1)


FULL_WIDTHS = (3 * BRANCH_W, 3 * BRANCH_W, 4 * BRANCH_W, D_MODEL, AB_PAD)


def kernel(x, c, ctx, c_ctx, norm_g, w_ada, b_ada, w_in, qkv_conv_w, a_log, dt_bias, gdn_norm_g,
           short_conv_w, conf_conv_w, conf_conv_b, conf_ln_g, conf_ln_b, smlp_ln_g, smlp_ln_b,
           smlp_w, smlp_b, w_out, final_g):
    bsz, seq, _ = x.shape
    depth = w_in.shape[0]
    assert bsz + 1 <= MOD_ROWS - 7 and seq % min(TOKEN_TILE, seq) == 0

    cvec = jnp.zeros((MOD_ROWS, D_MODEL), F32).at[:bsz].set(c).at[MOD_ROWS - 8].set(c_ctx)
    mod3 = _modulation(cvec, w_ada, b_ada).reshape(depth * MOD_ROWS, 1, 3 * D_MODEL)
    w_perm = _permute_w_in(w_in)
    w_out_b = w_out.astype(BF16)
    s_zero = jnp.zeros((bsz, BRANCH_W, BRANCH_W), F32)
    qkv_only = 3 * BRANCH_W

    xc = ctx
    for l in range(depth):
        last = l == depth - 1
        col_major = l % 2 == 1
        mix_args = (short_conv_w[l], conf_conv_w[l], conf_conv_b[l], conf_ln_g[l], conf_ln_b[l],
                    smlp_ln_g[l], smlp_ln_b[l], smlp_w[l], smlp_b[l])
        if last:
            w_ctx = jnp.concatenate([w_perm[l][:, :qkv_only], w_perm[l][:, -AB_PAD:]], axis=-1)
            pc_qkv, pc_ab = _in_projection(xc, mod3, norm_g[l], w_ctx, l, (qkv_only, AB_PAD), False, True)
        else:
            pc_qkv, pc_b, pc_cd, pc_gate, pc_ab = _in_projection(
                xc, mod3, norm_g[l], w_perm[l], l, FULL_WIDTHS, False, True)
        gdn_args = (qkv_conv_w[l], a_log[l], dt_bias[l], gdn_norm_g[l])
        yc, s_f, s_b = _mixer(pc_qkv, pc_ab, *gdn_args, s_zero, s_zero,
                              None if last else (pc_b, pc_cd) + mix_args)
        p_qkv, p_b, p_cd, p_gate, p_ab = _in_projection(
            x, mod3, norm_g[l], w_perm[l], l, FULL_WIDTHS, col_major, False)
        y, _, _ = _mixer(p_qkv, p_ab, *gdn_args, s_f, s_b, (p_b, p_cd) + mix_args)
        x = _out_projection(y, p_gate, w_out_b[l], mod3, x, l, col_major, False, final_g, last)
        if not last:
            xc = _out_projection(yc, pc_gate, w_out_b[l], mod3, xc, l, False, True, final_g, False)
    return x
```

```python
import functools

import jax
import jax.numpy as jnp
from jax import lax
from jax.experimental import pallas as pl
from jax.experimental.pallas import tpu as pltpu

F32 = jnp.float32
BF16 = jnp.bfloat16

D_MODEL = 1024
DEPTH = 4
GRID_W = 64
HEAD_DIM = 64
BRANCH_W = 256
N_HEADS = 4
A_COLS = 3 * BRANCH_W + 4 * N_HEADS
B_COLS = 3 * BRANCH_W
C_COLS = 2 * BRANCH_W
D_COLS = 2 * BRANCH_W
IN_COLS = A_COLS + B_COLS + C_COLS + D_COLS + D_MODEL
CONF_CONV = 31
GDN_CHUNK = 64
MLP_CHUNK = 128
EPS = 1e-6

LANES = 128
SUBLANES = 8
ROW_PACK = 16
AB_PAD = LANES
MOD_ROWS = 24
TOKEN_TILE = 1024
PERM_COLS = 8
CONV_PAD = 16
VMEM_LIMIT = 58 * 1024 * 1024
NEG_BIG = -1e30
INV_BASE = 32
GDN_INTERLEAVE = 4


def _sigmoid(x):
    return 1.0 / (1.0 + jnp.exp(-x))


def _silu(x):
    return x * _sigmoid(x)


def _softplus(x):
    return jnp.maximum(x, 0.0) + jnp.log1p(jnp.exp(-jnp.abs(x)))


def _dot(a, b):
    return jnp.dot(a.astype(BF16), b.astype(BF16), preferred_element_type=F32)


def _dot_nt(a, b):
    return lax.dot_general(a.astype(BF16), b.astype(BF16), (((1,), (1,)), ((), ())),
                           preferred_element_type=F32)


def _dot_tn(a, b):
    return lax.dot_general(a.astype(BF16), b.astype(BF16), (((0,), (0,)), ((), ())),
                           preferred_element_type=F32)


def _split3(x):
    x1 = x.astype(BF16)
    r1 = x - x1.astype(F32)
    x2 = r1.astype(BF16)
    x3 = (r1 - x2.astype(F32)).astype(BF16)
    return x1, x2, x3


def _mod_kernel(c_ref, w_ref, b_ref, o_ref):
    s = _silu(c_ref[...])
    o_ref[...] = jnp.dot(s, w_ref[...], preferred_element_type=F32) + b_ref[...]


def _modulation(cvec, w_ada, b_ada):
    depth = w_ada.shape[0]
    return pl.pallas_call(
        _mod_kernel,
        grid=(depth, 3),
        in_specs=[
            pl.BlockSpec((MOD_ROWS, D_MODEL), lambda l, k: (0, 0)),
            pl.BlockSpec((None, D_MODEL, D_MODEL), lambda l, k: (l, 0, k)),
            pl.BlockSpec((None, 1, D_MODEL), lambda l, k: (l, 0, k)),
        ],
        out_specs=pl.BlockSpec((None, MOD_ROWS, D_MODEL), lambda l, k: (l, 0, k)),
        out_shape=jax.ShapeDtypeStruct((depth, MOD_ROWS, 3 * D_MODEL), F32),
        compiler_params=pltpu.CompilerParams(
            dimension_semantics=("arbitrary", "arbitrary"), vmem_limit_bytes=VMEM_LIMIT),
        name="modulation",
    )(cvec, w_ada, b_ada.reshape(depth, 1, 3 * D_MODEL))


def _to_sequence_order(n_rows, n_cols, inverse=False):
    i = lax.broadcasted_iota(jnp.int32, (n_rows * n_cols, n_rows * n_cols), 0)
    j = lax.broadcasted_iota(jnp.int32, (n_rows * n_cols, n_rows * n_cols), 1)
    if inverse:
        i, j = j, i
    return j == (i % n_rows) * n_cols + i // n_rows


def _column_groups(ref):
    return [(slice(None), slice(g * PERM_COLS, (g + 1) * PERM_COLS), slice(None))
            for g in range(ref.shape[1] // PERM_COLS)]


def _inproj_kernel(x_ref, sh_ref, sc_ref, g_ref, w_ref, *out_refs, widths):
    def modulated(x):
        ms = jnp.mean(x * x, axis=-1, keepdims=True)
        h = x * lax.rsqrt(ms + EPS) * g_ref[...]
        return (h * (1.0 + sc_ref[...]) + sh_ref[...]).astype(BF16)

    if len(x_ref.shape) == 3:
        rows = x_ref.shape[0]
        perm = _to_sequence_order(rows, PERM_COLS).astype(BF16)
        hb = jnp.concatenate(
            [jnp.dot(perm, modulated(x_ref[idx].reshape(rows * PERM_COLS, D_MODEL)),
                     preferred_element_type=F32).astype(BF16) for idx in _column_groups(x_ref)], axis=0)
    else:
        hb = modulated(x_ref[...])
    off = 0
    for o_ref, wd in zip(out_refs, widths):
        o_ref[...] = jnp.dot(hb, w_ref[:, off:off + wd], preferred_element_type=F32).astype(o_ref.dtype)
        off += wd


def _mod_spec(layer, kind, ctx):
    if ctx:
        return pl.BlockSpec((None, 1, D_MODEL), lambda b, t: (layer * MOD_ROWS + MOD_ROWS - 8, 0, kind))
    return pl.BlockSpec((None, 1, D_MODEL), lambda b, t: (layer * MOD_ROWS + b, 0, kind))


def _token_spec(bsz, seq, width, col_major):
    tile = min(TOKEN_TILE, seq)
    if not col_major:
        return tile, (bsz, seq, width), pl.BlockSpec((None, tile, width), lambda b, t: (b, t, 0))
    rows = seq // GRID_W
    assert tile % (rows * PERM_COLS) == 0
    return (tile, (bsz, rows, GRID_W, width),
            pl.BlockSpec((None, rows, tile // rows, width), lambda b, t: (b, 0, t, 0)))


def _layer_weight_spec(w, layer):
    return pl.BlockSpec((None,) + w.shape[1:], lambda b, t: (layer, 0, 0))


def _in_projection(x, mod3, norm_g_l, w, w_layer, layer, widths, col_major, ctx):
    bsz, seq, _ = x.shape
    tile, view, x_spec = _token_spec(bsz, seq, D_MODEL, col_major)
    assert w.shape[2] == sum(widths)
    kern = functools.partial(_inproj_kernel, widths=tuple(widths))
    return pl.pallas_call(
        kern,
        grid=(bsz, seq // tile),
        in_specs=[x_spec] + [
            _mod_spec(layer, 0, ctx), _mod_spec(layer, 1, ctx),
            pl.BlockSpec((1, D_MODEL), lambda b, t: (0, 0)),
            _layer_weight_spec(w, w_layer),
        ],
        out_specs=[pl.BlockSpec((None, tile, wd), lambda b, t: (b, t, 0)) for wd in widths],
        out_shape=[jax.ShapeDtypeStruct((bsz, seq, wd), F32 if wd == AB_PAD else BF16) for wd in widths],
        compiler_params=pltpu.CompilerParams(
            dimension_semantics=("arbitrary", "arbitrary"), vmem_limit_bytes=VMEM_LIMIT),
        name="in_projection",
    )(x.reshape(view), mod3, mod3, norm_g_l.reshape(1, D_MODEL), w)


def _outproj_kernel(y_ref, pg_ref, w_ref, gate_ref, x_ref, fg_ref, o_ref, *, final):
    def residual(x, r):
        xn = x + gate_ref[...] * r
        if final:
            ms = jnp.mean(xn * xn, axis=-1, keepdims=True)
            xn = xn * lax.rsqrt(ms + EPS) * fg_ref[...]
        return xn

    z = (y_ref[...].astype(F32) * _silu(pg_ref[...].astype(F32))).astype(BF16)
    if len(x_ref.shape) == 3:
        rows = x_ref.shape[0]
        n = rows * PERM_COLS
        unperm = _to_sequence_order(rows, PERM_COLS, inverse=True).astype(BF16)
        groups = _column_groups(x_ref)
        z = jnp.concatenate(
            [jnp.dot(unperm, z[g * n:(g + 1) * n, :], preferred_element_type=F32).astype(BF16)
             for g in range(len(groups))], axis=0)
        r = jnp.dot(z, w_ref[...], preferred_element_type=F32)
        for g, idx in enumerate(groups):
            xn = residual(x_ref[idx].reshape(n, D_MODEL), r[g * n:(g + 1) * n, :])
            o_ref[idx] = xn.reshape(rows, PERM_COLS, D_MODEL)
    else:
        o_ref[...] = residual(x_ref[...], jnp.dot(z, w_ref[...], preferred_element_type=F32))


def _out_projection(y, p_gate, w_out, mod3, x, layer, col_major, ctx, final_g, final):
    bsz, seq, _ = x.shape
    tile, view, x_spec = _token_spec(bsz, seq, D_MODEL, col_major)
    tok = pl.BlockSpec((None, tile, D_MODEL), lambda b, t: (b, t, 0))
    out = pl.pallas_call(
        functools.partial(_outproj_kernel, final=final),
        grid=(bsz, seq // tile),
        in_specs=[tok, tok,
                  _layer_weight_spec(w_out, layer),
                  _mod_spec(layer, 2, ctx), x_spec,
                  pl.BlockSpec((1, D_MODEL), lambda b, t: (0, 0))],
        out_specs=x_spec,
        out_shape=jax.ShapeDtypeStruct(view, F32),
        compiler_params=pltpu.CompilerParams(
            dimension_semantics=("arbitrary", "arbitrary"), vmem_limit_bytes=VMEM_LIMIT),
        name="out_projection",
    )(y, p_gate, w_out, mod3, x.reshape(view), final_g.reshape(1, D_MODEL))
    return out.reshape(bsz, seq, D_MODEL)


def _mixer_kernel(*refs, bcd):
    pq_ref, pab_ref, cw_ref, alog_ref, dtb_ref, ng_ref, s0f_ref, s0b_ref = refs[:8]
    n_in = 8 + (N_BCD_INPUTS if bcd else 0)
    (y_ref, sf_ref, sb_ref, u_ref, w_ref, at_ref, kd_ref, qe_ref, gl_ref, o_ref, qkv_ref,
     gx_ref) = refs[n_in:n_in + 12]
    if bcd:
        bcd_stage, bcd_convs, bcd_smlp = _bcd_stages(*refs[8:n_in], y_ref, *refs[n_in + 12:])
    seq = pq_ref.shape[0]
    n_chunks = seq // GDN_CHUNK
    C = GDN_CHUNK
    W = BRANCH_W
    NG = 4 * N_HEADS

    ri = lax.broadcasted_iota(jnp.int32, (C, W), 0)
    ci = lax.broadcasted_iota(jnp.int32, (C, W), 1)
    cj = ci & (C - 1)
    low_incl = ri >= cj
    low_strict = ri > cj
    up_incl = ri <= cj
    up_strict = ri < cj
    eye_cat = (ri == cj).astype(F32)
    same_block = lambda bits: (ri >> bits) == (cj >> bits)
    base_bits = INV_BASE.bit_length() - 1
    base_block = same_block(base_bits)
    merge_levels = [same_block(b + 1) & ~same_block(b) for b in range(base_bits, GDN_CHUNK.bit_length() - 1)]
    r2 = lax.broadcasted_iota(jnp.int32, (W, W), 0)
    c2 = lax.broadcasted_iota(jnp.int32, (W, W), 1)
    blockmask = (r2 >> 6) == (c2 >> 6)
    ones_blk = blockmask.astype(BF16)
    e_r = lax.broadcasted_iota(jnp.int32, (NG, 4 * W), 0)
    e_c = lax.broadcasted_iota(jnp.int32, (NG, 4 * W), 1)
    expand = (e_r == ((e_c >> 8) * N_HEADS + ((e_c & (W - 1)) >> 6))).astype(BF16)
    lane = lax.broadcasted_iota(jnp.int32, (C, AB_PAD), 1)
    rown = lax.broadcasted_iota(jnp.int32, (C, AB_PAD), 0)
    row64 = lax.broadcasted_iota(jnp.int32, (C, 3 * W), 0)

    def tile_heads(x):
        return jnp.concatenate([x] * N_HEADS, axis=0)

    def bd(x):
        return jnp.where(blockmask, tile_heads(x), 0.0)

    def cat_mm(a_cat, b_cat):
        return _dot(a_cat, bd(b_cat))

    def head_sums(xs):
        hi = [x.astype(BF16) for x in xs]
        lo = [(x - h.astype(F32)).astype(BF16) for x, h in zip(xs, hi)]
        r = jnp.dot(jnp.concatenate(hi + lo, axis=0), ones_blk, preferred_element_type=F32)
        n, m = len(xs), xs[0].shape[0]
        return [r[i * m:(i + 1) * m] + r[(n + i) * m:(n + i + 1) * m] for i in range(n)]

    def chunk_cumsum(x, reverse):
        s = 1
        while s < C:
            if reverse:
                x = x + jnp.where(rown < C - s, pltpu.roll(x, C - s, 0), 0.0)
            else:
                x = x + jnp.where(rown >= s, pltpu.roll(x, s, 0), 0.0)
            s *= 2
        return x

    cw = cw_ref[...]

    def conv_act(c, dep):
        r0 = pl.multiple_of(c * C, C)
        cur = pq_ref[pl.ds(r0, C), :].astype(F32) + dep
        prev = pq_ref[pl.ds(pl.multiple_of(jnp.maximum(r0 - ROW_PACK, 0), ROW_PACK), ROW_PACK), :].astype(F32)
        nxt = pq_ref[pl.ds(pl.multiple_of(jnp.minimum(r0 + C, seq - ROW_PACK), ROW_PACK), ROW_PACK), :].astype(F32)
        prev_row = prev[ROW_PACK - 1:ROW_PACK, :] * jnp.where(c > 0, 1.0, 0.0)
        next_row = nxt[0:1, :] * jnp.where(c < n_chunks - 1, 1.0, 0.0)
        up = jnp.where(row64 == 0, prev_row, pltpu.roll(cur, 1, 0))
        dn = jnp.where(row64 == C - 1, next_row, pltpu.roll(cur, C - 1, 0))
        return _silu(up * cw[0:1, :] + cur * cw[1:2, :] + dn * cw[2:3, :])

    def gate_sums(c, dep):
        ab = pab_ref[pl.ds(pl.multiple_of(c * C, C), C), :] + dep
        gk = -jnp.exp(alog_ref[...]) * _softplus(ab + dtb_ref[...])
        gb = jnp.where(lane < 2 * N_HEADS, gk, _sigmoid(ab))
        nar = jnp.where(lane < N_HEADS, chunk_cumsum(gb, False),
                        jnp.where(lane < 2 * N_HEADS, chunk_cumsum(gb, True), gb))
        return _split3(nar[:, :NG])

    G = qkv_ref.shape[1]
    n_iter = n_chunks // G

    def front_end(i, slot):
        cs = [i * G + j for j in range(G)]
        acts, nars = [], []
        pieces = [functools.partial(lambda c, dep: acts.append(conv_act(c, dep)), c) for c in cs]
        pieces += [functools.partial(lambda c, dep: nars.append(gate_sums(c, dep)), c) for c in cs]
        if bcd:
            pieces += [functools.partial(bcd_stage, c) for c in cs]

        def finish():
            sums = [head_sums([a[:, :W] * a[:, :W], a[:, W:2 * W] * a[:, W:2 * W]]) for a in acts]
            for j, (a, s, n3) in enumerate(zip(acts, sums, nars)):
                qkv_ref[slot, j, :, :W] = a[:, :W] * lax.rsqrt(s[0] + EPS) * (HEAD_DIM ** -0.5)
                qkv_ref[slot, j, :, W:2 * W] = a[:, W:2 * W] * lax.rsqrt(s[1] + EPS)
                qkv_ref[slot, j, :, 2 * W:] = a[:, 2 * W:]
                gx_ref[slot, j] = sum(jnp.dot(n, expand, preferred_element_type=F32) for n in n3)

        return pieces, finish

    def prepare(i, slot):
        pieces, finish = front_end(i, slot)
        for piece in pieces:
            piece(0.0)
        finish()

    def parallel_step(i, slot, prepare_next):
        pieces, finish = front_end(jnp.minimum(i + 1, n_iter - 1), 1 - slot) if prepare_next else ([], None)

        def fill(stage_out):
            if pieces:
                pieces.pop(0)(stage_out[0][0:1, 0:1] * 0.0)

        cs = [i * G + j for j in range(G)]
        chains = [(j, d) for j in range(G) for d in range(2)]
        q = [qkv_ref[slot, j, :, :W] for j in range(G)]
        k = [qkv_ref[slot, j, :, W:2 * W] for j in range(G)]
        v = [qkv_ref[slot, j, :, 2 * W:] for j in range(G)]
        gx = [gx_ref[slot, j] for j in range(G)]
        prod = [_dot_nt(jnp.concatenate([q[j], k[j]], axis=0), bd(k[j])) for j in range(G)]
        fill(prod)
        gi = [gx[j][:, d * W:(d + 1) * W] for j, d in chains]
        beta = [gx[j][:, (2 + d) * W:(3 + d) * W] for j, d in chains]
        incl = [up_incl if d else low_incl for _, d in chains]
        strict = [up_strict if d else low_strict for _, d in chains]
        g_end = [g[0:1, :] if d else g[C - 1:C, :] for g, (_, d) in zip(gi, chains)]
        gj = [jnp.sum(g * eye_cat, axis=0, keepdims=True) for g in gi]
        decay = [jnp.exp(jnp.where(m, g - g2, NEG_BIG)) for m, g, g2 in zip(incl, gi, gj)]
        eg = [jnp.exp(g) for g in gi]
        lmat = [-jnp.where(m, prod[j][C:] * b * dc, 0.0)
                for m, (j, _), b, dc in zip(strict, chains, beta, decay)]
        lblk = [jnp.where(base_block, l, 0.0) for l in lmat]
        tinv = [eye_cat + l for l in lblk]
        power = [cat_mm(l, l) for l in lblk]
        fill(power)
        for _ in range(base_bits - 2):
            st = [cat_mm(jnp.concatenate([t, p], axis=0), p) for t, p in zip(tinv, power)]
            tinv = [t + s[:C] for t, s in zip(tinv, st)]
            power = [s[C:] for s in st]
            fill(st)
        st = [cat_mm(t, p) for t, p in zip(tinv, power)]
        tinv = [t + s for t, s in zip(tinv, st)]
        fill(st)
        for level in merge_levels:
            y = [cat_mm(jnp.where(level, l, 0.0), t) for l, t in zip(lmat, tinv)]
            fill(y)
            st = [cat_mm(t, x) for t, x in zip(tinv, y)]
            tinv = [t + s for t, s in zip(tinv, st)]
            fill(st)
        tinv = [t.astype(BF16) for t in tinv]
        u = [jnp.dot(t, bd(v[j] * b).astype(BF16), preferred_element_type=F32)
             for t, (j, _), b in zip(tinv, chains, beta)]
        fill(u)
        w = [jnp.dot(t, bd(k[j] * b * e).astype(BF16), preferred_element_type=F32)
             for t, (j, _), b, e in zip(tinv, chains, beta, eg)]
        fill(w)
        for n, (j, d) in enumerate(chains):
            rows = pl.ds(pl.multiple_of(cs[j] * C, C), C)
            u_ref[d, rows, :] = u[n].astype(BF16)
            w_ref[d, rows, :] = (-w[n]).astype(BF16)
            at_ref[d, rows, :] = jnp.where(incl[n], prod[j][:C] * decay[n], 0.0).astype(BF16)
            kd_ref[d, rows, :] = (k[j] * jnp.exp(g_end[n] - gi[n])).astype(BF16)
            qe_ref[d, rows, :] = (q[j] * eg[n]).astype(BF16)
            gl_ref[d, pl.ds(pl.multiple_of(cs[j] * SUBLANES, SUBLANES), SUBLANES), :] = jnp.broadcast_to(
                jnp.exp(g_end[n]), (SUBLANES, W))
        if prepare_next:
            while pieces:
                fill(w)
            finish()

    def parallel_pair(p, carry):
        parallel_step(2 * p, 0, True)
        parallel_step(2 * p + 1, 1, True)
        return carry

    prepare(0, 0)
    lax.fori_loop(0, n_iter // 2, parallel_pair, 0)
    if n_iter % 2:
        parallel_step(n_iter - 1, 0, False)

    maskb = blockmask.astype(BF16)

    def state_free(d, c):
        rows = pl.ds(pl.multiple_of(c * C, C), C)
        u, w, at, kd = u_ref[d, rows, :], w_ref[d, rows, :], at_ref[d, rows, :], kd_ref[d, rows, :]
        au = jnp.dot(at, tile_heads(u) * maskb, preferred_element_type=F32)
        aw = jnp.dot(at, tile_heads(w) * maskb, preferred_element_type=F32)
        n_all = _dot_tn(kd, u)
        m_all = _dot_tn(kd, w)
        lhs = jnp.concatenate([(qe_ref[d, rows, :].astype(F32) + aw).astype(BF16), m_all.astype(BF16)], axis=0)
        gl = gl_ref[d, pl.ds(pl.multiple_of(c * SUBLANES, SUBLANES), 1), :]
        return lhs, au, n_all, gl

    def state_step(parts, s):
        lhs, au, n_all, gl = parts
        t = jnp.dot(lhs, s.astype(BF16), preferred_element_type=F32)
        return t[:C] + au, s * gl + jnp.where(blockmask, t[C:] + n_all, 0.0)

    def scan(i, carry, first_visit):
        s_f, s_b = carry
        cb = n_chunks - 1 - i
        parts_f, parts_b = state_free(0, i), state_free(1, cb)
        o_f, s_f = state_step(parts_f, s_f)
        o_b, s_b = state_step(parts_b, s_b)
        rows_f = pl.ds(pl.multiple_of(i * C, C), C)
        rows_b = pl.ds(pl.multiple_of(cb * C, C), C)
        if first_visit:
            o_ref[rows_f, :] = o_f
            o_ref[rows_b, :] = o_b
        else:
            o_ref[rows_f, :] += o_f
            o_ref[rows_b, :] += o_b
        if bcd:
            bcd_convs(i)
        return s_f, s_b

    half = n_chunks // 2
    states = lax.fori_loop(0, half, functools.partial(scan, first_visit=True), (s0f_ref[...], s0b_ref[...]))
    s_f, s_b = lax.fori_loop(half, n_chunks, functools.partial(scan, first_visit=False), states)
    sf_ref[...] = s_f
    sb_ref[...] = s_b

    def finish(c, carry):
        rows = pl.ds(pl.multiple_of(c * W, W), W)
        o = o_ref[rows, :]
        ms = head_sums([o * o])[0] * (1.0 / HEAD_DIM)
        y_ref[rows, :W] = (o * lax.rsqrt(ms + EPS) * ng_ref[...]).astype(y_ref.dtype)
        if bcd:
            for n in range(W // MLP_CHUNK):
                bcd_smlp(c * (W // MLP_CHUNK) + n)
        return carry

    lax.fori_loop(0, seq // W, finish, 0)


def _mixer(p_qkv, p_ab, conv_w, a_log_l, dt_bias_l, gdn_norm_g_l, s0_f, s0_b, bcd_args=None):
    bsz, seq, _ = p_qkv.shape
    W = BRANCH_W
    n_chunks = seq // GDN_CHUNK
    interleave = min(GDN_INTERLEAVE, n_chunks)
    bcd = bcd_args is not None
    pad8 = jnp.zeros((AB_PAD - 2 * N_HEADS,), F32)
    alog_row = jnp.concatenate([a_log_l.reshape(-1), pad8]).reshape(1, AB_PAD)
    dtb_row = jnp.concatenate([dt_bias_l.reshape(-1), pad8]).reshape(1, AB_PAD)
    ng_row = jnp.tile(gdn_norm_g_l, N_HEADS).reshape(1, W)
    per_b = lambda r, wd: pl.BlockSpec((None, r, wd), lambda b: (b, 0, 0))
    const = lambda r, wd: pl.BlockSpec((r, wd), lambda b: (0, 0))
    in_specs = [per_b(seq, 3 * W), per_b(seq, AB_PAD), const(3, 3 * W), const(1, AB_PAD),
                const(1, AB_PAD), const(1, W), per_b(W, W), per_b(W, W)]
    args = [p_qkv, p_ab, conv_w, alog_row, dtb_row, ng_row, s0_f, s0_b]
    scratch = [
        pltpu.VMEM((2, seq, W), BF16),
        pltpu.VMEM((2, seq, W), BF16),
        pltpu.VMEM((2, seq, W), BF16),
        pltpu.VMEM((2, seq, W), BF16),
        pltpu.VMEM((2, seq, W), BF16),
        pltpu.VMEM((2, n_chunks * SUBLANES, W), F32),
        pltpu.VMEM((seq, W), F32),
        pltpu.VMEM((2, interleave, GDN_CHUNK, 3 * W), F32),
        pltpu.VMEM((2, interleave, GDN_CHUNK, 4 * W), F32),
    ]
    if bcd:
        p_b, p_cd, short_w, conf_w, conf_b, conf_g, conf_beta, smlp_g, smlp_beta, smlp_w, smlp_b = bcd_args
        row = lambda a: a.reshape(1, W)
        ws_rows = smlp_w.reshape(N_HEADS * MLP_CHUNK, MLP_CHUNK)
        bs_exp = jnp.repeat(smlp_b.T, HEAD_DIM, axis=1)
        in_specs += [per_b(seq, 3 * W), per_b(seq, 4 * W), const(3, W), const(CONF_CONV, W)] + [const(1, W)] * 5
        in_specs += [const(N_HEADS * MLP_CHUNK, MLP_CHUNK), const(MLP_CHUNK, W)]
        args += [p_b, p_cd, short_w, conf_w, row(conf_b), row(conf_g), row(conf_beta), row(smlp_g),
                 row(smlp_beta), ws_rows, bs_exp]
        assert len(args) == 8 + N_BCD_INPUTS
        scratch += [pltpu.VMEM((seq + 2 * CONV_PAD, W), F32),
                    pltpu.VMEM((seq, W), BF16)]
    y_w = 4 * W if bcd else W
    return pl.pallas_call(
        functools.partial(_mixer_kernel, bcd=bcd),
        grid=(bsz,),
        in_specs=in_specs,
        out_specs=[per_b(seq, y_w), per_b(W, W), per_b(W, W)],
        out_shape=[jax.ShapeDtypeStruct((bsz, seq, y_w), BF16),
                   jax.ShapeDtypeStruct((bsz, W, W), F32),
                   jax.ShapeDtypeStruct((bsz, W, W), F32)],
        scratch_shapes=scratch,
        compiler_params=pltpu.CompilerParams(
            dimension_semantics=("arbitrary",), vmem_limit_bytes=VMEM_LIMIT),
        name="mixer",
    )(*args)


def _layernorm(x, g, b):
    mu = jnp.mean(x, axis=-1, keepdims=True)
    xc = x - mu
    var = jnp.mean(xc * xc, axis=-1, keepdims=True)
    return xc * lax.rsqrt(var + EPS) * g + b


N_BCD_INPUTS = 11


def _bcd_stages(pb_ref, pcd_ref, sw_ref, cw_ref, cb_ref, clg_ref, clb_ref, slg_ref, slb_ref,
                ws_ref, bs_ref, y_ref, z_ref, vln_ref):
    seq = pb_ref.shape[0]
    W = BRANCH_W
    T = GDN_CHUNK
    n_tiles = seq // T
    row = lax.broadcasted_iota(jnp.int32, (T, W), 0)
    sw = sw_ref[...]
    cw = cw_ref[...]

    z_ref[0:CONV_PAD, :] = jnp.zeros((CONV_PAD, W), F32)
    z_ref[CONV_PAD + seq:, :] = jnp.zeros((CONV_PAD, W), F32)

    def stage(t, dep):
        r0 = pl.multiple_of(t * T, T)
        cd = pcd_ref[pl.ds(r0, T), :].astype(F32) + dep
        z_ref[pl.ds(pl.multiple_of(r0 + CONV_PAD, SUBLANES), T), :] = cd[:, :W] * _sigmoid(cd[:, W:2 * W])
        vln_ref[pl.ds(r0, T), :] = _layernorm(cd[:, 3 * W:], slg_ref[...], slb_ref[...]).astype(vln_ref.dtype)

    def convs(t):
        r0 = pl.multiple_of(t * T, T)
        pb = pb_ref[pl.ds(r0, T), :].astype(F32)
        cur = pb[:, W:2 * W] * pb[:, 2 * W:]
        pv = pb_ref[pl.ds(pl.multiple_of(jnp.maximum(r0 - ROW_PACK, 0), ROW_PACK), ROW_PACK), :].astype(F32)
        nx = pb_ref[pl.ds(pl.multiple_of(jnp.minimum(r0 + T, seq - ROW_PACK), ROW_PACK), ROW_PACK), :].astype(F32)
        prev_row = (pv[:, W:2 * W] * pv[:, 2 * W:])[ROW_PACK - 1:ROW_PACK, :] * jnp.where(t > 0, 1.0, 0.0)
        next_row = (nx[:, W:2 * W] * nx[:, 2 * W:])[0:1, :] * jnp.where(t < n_tiles - 1, 1.0, 0.0)
        up = jnp.where(row == 0, prev_row, pltpu.roll(cur, 1, 0))
        dn = jnp.where(row == T - 1, next_row, pltpu.roll(cur, T - 1, 0))
        y_ref[pl.ds(r0, T), W:2 * W] = (pb[:, :W] * (up * sw[0:1, :] + cur * sw[1:2, :] + dn * sw[2:3, :])
                                        ).astype(y_ref.dtype)
        acc = jnp.zeros((T, W), F32)
        for r in range(SUBLANES):
            part = None
            for a in range(4):
                o = SUBLANES * a + r
                if o < 1 or o > CONF_CONV:
                    continue
                win = z_ref[pl.ds(pl.multiple_of(r0 + SUBLANES * a, SUBLANES), T + SUBLANES), :]
                term = win * cw[o - 1:o, :]
                part = term if part is None else part + term
            if r:
                part = pltpu.roll(part, T + SUBLANES - r, 0)
            acc = acc + part[:T, :]
        zc = _layernorm(acc + cb_ref[...], clg_ref[...], clb_ref[...])
        y_ref[pl.ds(r0, T), 2 * W:3 * W] = _silu(zc).astype(y_ref.dtype)

    lane_grp = lax.broadcasted_iota(jnp.int32, (MLP_CHUNK, W), 1) >> 6
    wsb = ws_ref[...].astype(BF16)

    def smlp(n):
        r0 = pl.multiple_of(n * MLP_CHUNK, MLP_CHUNK)
        full = jnp.dot(wsb, vln_ref[pl.ds(r0, MLP_CHUNK), :].astype(BF16), preferred_element_type=F32)
        mixed = bs_ref[...]
        for g in range(N_HEADS):
            mixed = mixed + jnp.where(lane_grp == g, full[g * MLP_CHUNK:(g + 1) * MLP_CHUNK, :], 0.0)
        u = pcd_ref[pl.ds(r0, MLP_CHUNK), 2 * W:3 * W].astype(F32)
        y_ref[pl.ds(r0, MLP_CHUNK), 3 * W:] = (u * mixed).astype(y_ref.dtype)

    return stage, convs, smlp


def _permute_w_in(w_in):
    depth = w_in.shape[0]
    b0 = A_COLS
    c0 = b0 + B_COLS
    g0 = c0 + C_COLS + D_COLS
    w_in = w_in.astype(BF16)
    pad = jnp.zeros((depth, D_MODEL, AB_PAD - 4 * N_HEADS), BF16)
    return jnp.concatenate(
        [w_in[:, :, :3 * BRANCH_W], w_in[:, :, b0:c0], w_in[:, :, c0:g0], w_in[:, :, g0:],
         w_in[:, :, 3 * BRANCH_W:A_COLS], pad], axis=-1)


FULL_WIDTHS = (3 * BRANCH_W, 3 * BRANCH_W, 4 * BRANCH_W, D_MODEL, AB_PAD)


def kernel(x, c, ctx, c_ctx, norm_g, w_ada, b_ada, w_in, qkv_conv_w, a_log, dt_bias, gdn_norm_g,
           short_conv_w, conf_conv_w, conf_conv_b, conf_ln_g, conf_ln_b, smlp_ln_g, smlp_ln_b,
           smlp_w, smlp_b, w_out, final_g):
    bsz, seq, _ = x.shape
    depth = w_in.shape[0]
    assert bsz + 1 <= MOD_ROWS - 7 and seq % min(TOKEN_TILE, seq) == 0

    cvec = jnp.zeros((MOD_ROWS, D_MODEL), F32).at[:bsz].set(c).at[MOD_ROWS - 8].set(c_ctx)
    mod3 = _modulation(cvec, w_ada, b_ada).reshape(depth * MOD_ROWS, 1, 3 * D_MODEL)
    w_perm = _permute_w_in(w_in)
    w_out_b = w_out.astype(BF16)
    s_zero = jnp.zeros((bsz, BRANCH_W, BRANCH_W), F32)
    qkv_only = 3 * BRANCH_W

    xc = ctx
    for l in range(depth):
        last = l == depth - 1
        col_major = l % 2 == 1
        mix_args = (short_conv_w[l], conf_conv_w[l], conf_conv_b[l], conf_ln_g[l], conf_ln_b[l],
                    smlp_ln_g[l], smlp_ln_b[l], smlp_w[l], smlp_b[l])
        if last:
            w_ctx = jnp.concatenate([w_perm[l:, :, :qkv_only], w_perm[l:, :, -AB_PAD:]], axis=-1)
            pc_qkv, pc_ab = _in_projection(xc, mod3, norm_g[l], w_ctx, 0, l, (qkv_only, AB_PAD), False, True)
        else:
            pc_qkv, pc_b, pc_cd, pc_gate, pc_ab = _in_projection(
                xc, mod3, norm_g[l], w_perm, l, l, FULL_WIDTHS, False, True)
        gdn_args = (qkv_conv_w[l], a_log[l], dt_bias[l], gdn_norm_g[l])
        yc, s_f, s_b = _mixer(pc_qkv, pc_ab, *gdn_args, s_zero, s_zero,
                              None if last else (pc_b, pc_cd) + mix_args)
        p_qkv, p_b, p_cd, p_gate, p_ab = _in_projection(
            x, mod3, norm_g[l], w_perm, l, l, FULL_WIDTHS, col_major, False)
        y, _, _ = _mixer(p_qkv, p_ab, *gdn_args, s_f, s_b, (p_b, p_cd) + mix_args)
        x = _out_projection(y, p_gate, w_out_b, mod3, x, l, col_major, False, final_g, last)
        if not last:
            xc = _out_projection(yc, pc_gate, w_out_b, mod3, xc, l, False, True, final_g, False)
    return x
```

```python
import functools

import jax
import jax.numpy as jnp
from jax import lax
from jax.experimental import pallas as pl
from jax.experimental.pallas import tpu as pltpu

F32 = jnp.float32
BF16 = jnp.bfloat16

D_MODEL = 1024
DEPTH = 4
GRID_W = 64
HEAD_DIM = 64
BRANCH_W = 256
N_HEADS = 4
A_COLS = 3 * BRANCH_W + 4 * N_HEADS
B_COLS = 3 * BRANCH_W
C_COLS = 2 * BRANCH_W
D_COLS = 2 * BRANCH_W
IN_COLS = A_COLS + B_COLS + C_COLS + D_COLS + D_MODEL
CONF_CONV = 31
GDN_CHUNK = 64
MLP_CHUNK = 128
EPS = 1e-6

LANES = 128
SUBLANES = 8
ROW_PACK = 16
AB_PAD = LANES
MOD_ROWS = 24
TOKEN_TILE = 1024
PERM_COLS = 8
CONV_PAD = 16
VMEM_LIMIT = 58 * 1024 * 1024
NEG_BIG = -1e30
INV_BASE = 32
GDN_INTERLEAVE = 4


def _sigmoid(x):
    return 1.0 / (1.0 + jnp.exp(-x))


def _silu(x):
    return x * _sigmoid(x)


def _softplus(x):
    return jnp.maximum(x, 0.0) + jnp.log1p(jnp.exp(-jnp.abs(x)))


def _dot(a, b):
    return jnp.dot(a.astype(BF16), b.astype(BF16), preferred_element_type=F32)


def _dot_nt(a, b):
    return lax.dot_general(a.astype(BF16), b.astype(BF16), (((1,), (1,)), ((), ())),
                           preferred_element_type=F32)


def _dot_tn(a, b):
    return lax.dot_general(a.astype(BF16), b.astype(BF16), (((0,), (0,)), ((), ())),
                           preferred_element_type=F32)


def _split3(x):
    x1 = x.astype(BF16)
    r1 = x - x1.astype(F32)
    x2 = r1.astype(BF16)
    x3 = (r1 - x2.astype(F32)).astype(BF16)
    return x1, x2, x3


def _mod_kernel(c_ref, w_ref, b_ref, o_ref):
    s = _silu(c_ref[...])
    o_ref[...] = jnp.dot(s, w_ref[...], preferred_element_type=F32) + b_ref[...]


def _modulation(cvec, w_ada, b_ada):
    depth = w_ada.shape[0]
    return pl.pallas_call(
        _mod_kernel,
        grid=(depth, 3),
        in_specs=[
            pl.BlockSpec((MOD_ROWS, D_MODEL), lambda l, k: (0, 0)),
            pl.BlockSpec((None, D_MODEL, D_MODEL), lambda l, k: (l, 0, k)),
            pl.BlockSpec((None, 1, D_MODEL), lambda l, k: (l, 0, k)),
        ],
        out_specs=pl.BlockSpec((None, MOD_ROWS, D_MODEL), lambda l, k: (l, 0, k)),
        out_shape=jax.ShapeDtypeStruct((depth, MOD_ROWS, 3 * D_MODEL), F32),
        compiler_params=pltpu.CompilerParams(
            dimension_semantics=("arbitrary", "arbitrary"), vmem_limit_bytes=VMEM_LIMIT),
        name="modulation",
    )(cvec, w_ada, b_ada.reshape(depth, 1, 3 * D_MODEL))


def _to_sequence_order(n_rows, n_cols, inverse=False):
    i = lax.broadcasted_iota(jnp.int32, (n_rows * n_cols, n_rows * n_cols), 0)
    j = lax.broadcasted_iota(jnp.int32, (n_rows * n_cols, n_rows * n_cols), 1)
    if inverse:
        i, j = j, i
    return j == (i % n_rows) * n_cols + i // n_rows


def _column_groups(ref):
    return [(slice(None), slice(g * PERM_COLS, (g + 1) * PERM_COLS), slice(None))
            for g in range(ref.shape[1] // PERM_COLS)]


def _inproj_kernel(x_ref, sh_ref, sc_ref, g_ref, *refs):
    w_refs, out_refs = refs[:len(refs) // 2], refs[len(refs) // 2:]

    def modulated(x):
        ms = jnp.mean(x * x, axis=-1, keepdims=True)
        h = x * lax.rsqrt(ms + EPS) * g_ref[...]
        return (h * (1.0 + sc_ref[...]) + sh_ref[...]).astype(BF16)

    if len(x_ref.shape) == 3:
        rows = x_ref.shape[0]
        perm = _to_sequence_order(rows, PERM_COLS).astype(BF16)
        hb = jnp.concatenate(
            [jnp.dot(perm, modulated(x_ref[idx].reshape(rows * PERM_COLS, D_MODEL)),
                     preferred_element_type=F32).astype(BF16) for idx in _column_groups(x_ref)], axis=0)
    else:
        hb = modulated(x_ref[...])
    for w_ref, o_ref in zip(w_refs, out_refs):
        o_ref[...] = jnp.dot(hb, w_ref[...], preferred_element_type=F32).astype(o_ref.dtype)


def _mod_spec(layer, kind, ctx):
    if ctx:
        return pl.BlockSpec((None, 1, D_MODEL), lambda b, t: (layer * MOD_ROWS + MOD_ROWS - 8, 0, kind))
    return pl.BlockSpec((None, 1, D_MODEL), lambda b, t: (layer * MOD_ROWS + b, 0, kind))


def _token_spec(bsz, seq, width, col_major):
    tile = min(TOKEN_TILE, seq)
    if not col_major:
        return tile, (bsz, seq, width), pl.BlockSpec((None, tile, width), lambda b, t: (b, t, 0))
    rows = seq // GRID_W
    assert tile % (rows * PERM_COLS) == 0
    return (tile, (bsz, rows, GRID_W, width),
            pl.BlockSpec((None, rows, tile // rows, width), lambda b, t: (b, 0, t, 0)))


def _layer_weight_spec(w, layer):
    return pl.BlockSpec((None,) + w.shape[1:], lambda b, t: (layer, 0, 0))


def _in_projection(x, mod3, norm_g_l, ws, layer, col_major, ctx):
    bsz, seq, _ = x.shape
    tile, view, x_spec = _token_spec(bsz, seq, D_MODEL, col_major)
    widths = [w.shape[2] for w in ws]
    return pl.pallas_call(
        _inproj_kernel,
        grid=(bsz, seq // tile),
        in_specs=[x_spec] + [
            _mod_spec(layer, 0, ctx), _mod_spec(layer, 1, ctx),
            pl.BlockSpec((1, D_MODEL), lambda b, t: (0, 0)),
        ] + [_layer_weight_spec(w, layer) for w in ws],
        out_specs=[pl.BlockSpec((None, tile, wd), lambda b, t: (b, t, 0)) for wd in widths],
        out_shape=[jax.ShapeDtypeStruct((bsz, seq, wd), F32 if wd == AB_PAD else BF16) for wd in widths],
        compiler_params=pltpu.CompilerParams(
            dimension_semantics=("arbitrary", "arbitrary"), vmem_limit_bytes=VMEM_LIMIT),
        name="in_projection",
    )(x.reshape(view), mod3, mod3, norm_g_l.reshape(1, D_MODEL), *ws)


def _outproj_kernel(y_ref, pg_ref, w_ref, gate_ref, x_ref, fg_ref, o_ref, *, final):
    def residual(x, r):
        xn = x + gate_ref[...] * r
        if final:
            ms = jnp.mean(xn * xn, axis=-1, keepdims=True)
            xn = xn * lax.rsqrt(ms + EPS) * fg_ref[...]
        return xn

    z = (y_ref[...].astype(F32) * _silu(pg_ref[...].astype(F32))).astype(BF16)
    if len(x_ref.shape) == 3:
        rows = x_ref.shape[0]
        n = rows * PERM_COLS
        unperm = _to_sequence_order(rows, PERM_COLS, inverse=True).astype(BF16)
        groups = _column_groups(x_ref)
        z = jnp.concatenate(
            [jnp.dot(unperm, z[g * n:(g + 1) * n, :], preferred_element_type=F32).astype(BF16)
             for g in range(len(groups))], axis=0)
        r = jnp.dot(z, w_ref[...], preferred_element_type=F32)
        for g, idx in enumerate(groups):
            xn = residual(x_ref[idx].reshape(n, D_MODEL), r[g * n:(g + 1) * n, :])
            o_ref[idx] = xn.reshape(rows, PERM_COLS, D_MODEL)
    else:
        o_ref[...] = residual(x_ref[...], jnp.dot(z, w_ref[...], preferred_element_type=F32))


def _out_projection(y, p_gate, w_out, mod3, x, layer, col_major, ctx, final_g, final):
    bsz, seq, _ = x.shape
    tile, view, x_spec = _token_spec(bsz, seq, D_MODEL, col_major)
    tok = pl.BlockSpec((None, tile, D_MODEL), lambda b, t: (b, t, 0))
    out = pl.pallas_call(
        functools.partial(_outproj_kernel, final=final),
        grid=(bsz, seq // tile),
        in_specs=[tok, tok,
                  _layer_weight_spec(w_out, layer),
                  _mod_spec(layer, 2, ctx), x_spec,
                  pl.BlockSpec((1, D_MODEL), lambda b, t: (0, 0))],
        out_specs=x_spec,
        out_shape=jax.ShapeDtypeStruct(view, F32),
        compiler_params=pltpu.CompilerParams(
            dimension_semantics=("arbitrary", "arbitrary"), vmem_limit_bytes=VMEM_LIMIT),
        name="out_projection",
    )(y, p_gate, w_out, mod3, x.reshape(view), final_g.reshape(1, D_MODEL))
    return out.reshape(bsz, seq, D_MODEL)


def _mixer_kernel(*refs, bcd):
    pq_ref, pab_ref, cw_ref, alog_ref, dtb_ref, ng_ref, s0f_ref, s0b_ref = refs[:8]
    n_in = 8 + (N_BCD_INPUTS if bcd else 0)
    (y_ref, sf_ref, sb_ref, u_ref, w_ref, at_ref, kd_ref, qe_ref, gl_ref, o_ref, qkv_ref,
     gx_ref) = refs[n_in:n_in + 12]
    if bcd:
        bcd_stage, bcd_convs, bcd_smlp = _bcd_stages(*refs[8:n_in], y_ref, *refs[n_in + 12:])
    seq = pq_ref.shape[0]
    n_chunks = seq // GDN_CHUNK
    C = GDN_CHUNK
    W = BRANCH_W
    NG = 4 * N_HEADS

    ri = lax.broadcasted_iota(jnp.int32, (C, W), 0)
    ci = lax.broadcasted_iota(jnp.int32, (C, W), 1)
    cj = ci & (C - 1)
    low_incl = ri >= cj
    low_strict = ri > cj
    up_incl = ri <= cj
    up_strict = ri < cj
    eye_cat = (ri == cj).astype(F32)
    same_block = lambda bits: (ri >> bits) == (cj >> bits)
    base_bits = INV_BASE.bit_length() - 1
    base_block = same_block(base_bits)
    merge_levels = [same_block(b + 1) & ~same_block(b) for b in range(base_bits, GDN_CHUNK.bit_length() - 1)]
    r2 = lax.broadcasted_iota(jnp.int32, (W, W), 0)
    c2 = lax.broadcasted_iota(jnp.int32, (W, W), 1)
    blockmask = (r2 >> 6) == (c2 >> 6)
    ones_blk = blockmask.astype(BF16)
    e_r = lax.broadcasted_iota(jnp.int32, (NG, 4 * W), 0)
    e_c = lax.broadcasted_iota(jnp.int32, (NG, 4 * W), 1)
    expand = (e_r == ((e_c >> 8) * N_HEADS + ((e_c & (W - 1)) >> 6))).astype(BF16)
    lane = lax.broadcasted_iota(jnp.int32, (C, AB_PAD), 1)
    rown = lax.broadcasted_iota(jnp.int32, (C, AB_PAD), 0)
    row64 = lax.broadcasted_iota(jnp.int32, (C, 3 * W), 0)

    def tile_heads(x):
        return jnp.concatenate([x] * N_HEADS, axis=0)

    def bd(x):
        return jnp.where(blockmask, tile_heads(x), 0.0)

    def cat_mm(a_cat, b_cat):
        return _dot(a_cat, bd(b_cat))

    def head_sums(xs):
        hi = [x.astype(BF16) for x in xs]
        lo = [(x - h.astype(F32)).astype(BF16) for x, h in zip(xs, hi)]
        r = jnp.dot(jnp.concatenate(hi + lo, axis=0), ones_blk, preferred_element_type=F32)
        n, m = len(xs), xs[0].shape[0]
        return [r[i * m:(i + 1) * m] + r[(n + i) * m:(n + i + 1) * m] for i in range(n)]

    def chunk_cumsum(x, reverse):
        s = 1
        while s < C:
            if reverse:
                x = x + jnp.where(rown < C - s, pltpu.roll(x, C - s, 0), 0.0)
            else:
                x = x + jnp.where(rown >= s, pltpu.roll(x, s, 0), 0.0)
            s *= 2
        return x

    cw = cw_ref[...]

    def conv_act(c, dep):
        r0 = pl.multiple_of(c * C, C)
        cur = pq_ref[pl.ds(r0, C), :].astype(F32) + dep
        prev = pq_ref[pl.ds(pl.multiple_of(jnp.maximum(r0 - ROW_PACK, 0), ROW_PACK), ROW_PACK), :].astype(F32)
        nxt = pq_ref[pl.ds(pl.multiple_of(jnp.minimum(r0 + C, seq - ROW_PACK), ROW_PACK), ROW_PACK), :].astype(F32)
        prev_row = prev[ROW_PACK - 1:ROW_PACK, :] * jnp.where(c > 0, 1.0, 0.0)
        next_row = nxt[0:1, :] * jnp.where(c < n_chunks - 1, 1.0, 0.0)
        up = jnp.where(row64 == 0, prev_row, pltpu.roll(cur, 1, 0))
        dn = jnp.where(row64 == C - 1, next_row, pltpu.roll(cur, C - 1, 0))
        return _silu(up * cw[0:1, :] + cur * cw[1:2, :] + dn * cw[2:3, :])

    def gate_sums(c, dep):
        ab = pab_ref[pl.ds(pl.multiple_of(c * C, C), C), :] + dep
        gk = -jnp.exp(alog_ref[...]) * _softplus(ab + dtb_ref[...])
        gb = jnp.where(lane < 2 * N_HEADS, gk, _sigmoid(ab))
        nar = jnp.where(lane < N_HEADS, chunk_cumsum(gb, False),
                        jnp.where(lane < 2 * N_HEADS, chunk_cumsum(gb, True), gb))
        return _split3(nar[:, :NG])

    G = qkv_ref.shape[1]
    n_iter = n_chunks // G

    def front_end(i, slot):
        cs = [i * G + j for j in range(G)]
        acts, nars = [], []
        pieces = [functools.partial(lambda c, dep: acts.append(conv_act(c, dep)), c) for c in cs]
        pieces += [functools.partial(lambda c, dep: nars.append(gate_sums(c, dep)), c) for c in cs]
        if bcd:
            pieces += [functools.partial(bcd_stage, c) for c in cs]

        def finish():
            sums = [head_sums([a[:, :W] * a[:, :W], a[:, W:2 * W] * a[:, W:2 * W]]) for a in acts]
            for j, (a, s, n3) in enumerate(zip(acts, sums, nars)):
                qkv_ref[slot, j, :, :W] = a[:, :W] * lax.rsqrt(s[0] + EPS) * (HEAD_DIM ** -0.5)
                qkv_ref[slot, j, :, W:2 * W] = a[:, W:2 * W] * lax.rsqrt(s[1] + EPS)
                qkv_ref[slot, j, :, 2 * W:] = a[:, 2 * W:]
                gx_ref[slot, j] = sum(jnp.dot(n, expand, preferred_element_type=F32) for n in n3)

        return pieces, finish

    def prepare(i, slot):
        pieces, finish = front_end(i, slot)
        for piece in pieces:
            piece(0.0)
        finish()

    def parallel_step(i, slot, prepare_next):
        pieces, finish = front_end(jnp.minimum(i + 1, n_iter - 1), 1 - slot) if prepare_next else ([], None)

        def fill(stage_out):
            if pieces:
                pieces.pop(0)(stage_out[0][0:1, 0:1] * 0.0)

        cs = [i * G + j for j in range(G)]
        chains = [(j, d) for j in range(G) for d in range(2)]
        q = [qkv_ref[slot, j, :, :W] for j in range(G)]
        k = [qkv_ref[slot, j, :, W:2 * W] for j in range(G)]
        v = [qkv_ref[slot, j, :, 2 * W:] for j in range(G)]
        gx = [gx_ref[slot, j] for j in range(G)]
        prod = [_dot_nt(jnp.concatenate([q[j], k[j]], axis=0), bd(k[j])) for j in range(G)]
        fill(prod)
        gi = [gx[j][:, d * W:(d + 1) * W] for j, d in chains]
        beta = [gx[j][:, (2 + d) * W:(3 + d) * W] for j, d in chains]
        incl = [up_incl if d else low_incl for _, d in chains]
        strict = [up_strict if d else low_strict for _, d in chains]
        g_end = [g[0:1, :] if d else g[C - 1:C, :] for g, (_, d) in zip(gi, chains)]
        gj = [jnp.sum(g * eye_cat, axis=0, keepdims=True) for g in gi]
        decay = [jnp.exp(jnp.where(m, g - g2, NEG_BIG)) for m, g, g2 in zip(incl, gi, gj)]
        eg = [jnp.exp(g) for g in gi]
        lmat = [-jnp.where(m, prod[j][C:] * b * dc, 0.0)
                for m, (j, _), b, dc in zip(strict, chains, beta, decay)]
        lblk = [jnp.where(base_block, l, 0.0) for l in lmat]
        tinv = [eye_cat + l for l in lblk]
        power = [cat_mm(l, l) for l in lblk]
        fill(power)
        for _ in range(base_bits - 2):
            st = [cat_mm(jnp.concatenate([t, p], axis=0), p) for t, p in zip(tinv, power)]
            tinv = [t + s[:C] for t, s in zip(tinv, st)]
            power = [s[C:] for s in st]
            fill(st)
        st = [cat_mm(t, p) for t, p in zip(tinv, power)]
        tinv = [t + s for t, s in zip(tinv, st)]
        fill(st)
        for level in merge_levels:
            y = [cat_mm(jnp.where(level, l, 0.0), t) for l, t in zip(lmat, tinv)]
            fill(y)
            st = [cat_mm(t, x) for t, x in zip(tinv, y)]
            tinv = [t + s for t, s in zip(tinv, st)]
            fill(st)
        tinv = [t.astype(BF16) for t in tinv]
        u = [jnp.dot(t, bd(v[j] * b).astype(BF16), preferred_element_type=F32)
             for t, (j, _), b in zip(tinv, chains, beta)]
        fill(u)
        w = [jnp.dot(t, bd(k[j] * b * e).astype(BF16), preferred_element_type=F32)
             for t, (j, _), b, e in zip(tinv, chains, beta, eg)]
        fill(w)
        for n, (j, d) in enumerate(chains):
            rows = pl.ds(pl.multiple_of(cs[j] * C, C), C)
            u_ref[d, rows, :] = u[n].astype(BF16)
            w_ref[d, rows, :] = (-w[n]).astype(BF16)
            at_ref[d, rows, :] = jnp.where(incl[n], prod[j][:C] * decay[n], 0.0).astype(BF16)
            kd_ref[d, rows, :] = (k[j] * jnp.exp(g_end[n] - gi[n])).astype(BF16)
            qe_ref[d, rows, :] = (q[j] * eg[n]).astype(BF16)
            gl_ref[d, pl.ds(pl.multiple_of(cs[j] * SUBLANES, SUBLANES), SUBLANES), :] = jnp.broadcast_to(
                jnp.exp(g_end[n]), (SUBLANES, W))
        if prepare_next:
            while pieces:
                fill(w)
            finish()

    def parallel_pair(p, carry):
        parallel_step(2 * p, 0, True)
        parallel_step(2 * p + 1, 1, True)
        return carry

    prepare(0, 0)
    lax.fori_loop(0, n_iter // 2, parallel_pair, 0)
    if n_iter % 2:
        parallel_step(n_iter - 1, 0, False)

    def state_free(d, c):
        rows = pl.ds(pl.multiple_of(c * C, C), C)
        u, w, at, kd = u_ref[d, rows, :], w_ref[d, rows, :], at_ref[d, rows, :], kd_ref[d, rows, :]
        au = jnp.dot(at, bd(u), preferred_element_type=F32)
        aw = jnp.dot(at, bd(w), preferred_element_type=F32)
        n_all = _dot_tn(kd, u)
        m_all = _dot_tn(kd, w)
        lhs = jnp.concatenate([(qe_ref[d, rows, :].astype(F32) + aw).astype(BF16), m_all.astype(BF16)], axis=0)
        gl = gl_ref[d, pl.ds(pl.multiple_of(c * SUBLANES, SUBLANES), 1), :]
        return lhs, au, n_all, gl

    def state_step(parts, s):
        lhs, au, n_all, gl = parts
        t = jnp.dot(lhs, s.astype(BF16), preferred_element_type=F32)
        return t[:C] + au, s * gl + jnp.where(blockmask, t[C:] + n_all, 0.0)

    def scan(i, carry, first_visit):
        s_f, s_b = carry
        cb = n_chunks - 1 - i
        parts_f, parts_b = state_free(0, i), state_free(1, cb)
        o_f, s_f = state_step(parts_f, s_f)
        o_b, s_b = state_step(parts_b, s_b)
        rows_f = pl.ds(pl.multiple_of(i * C, C), C)
        rows_b = pl.ds(pl.multiple_of(cb * C, C), C)
        if first_visit:
            o_ref[rows_f, :] = o_f
            o_ref[rows_b, :] = o_b
        else:
            o_ref[rows_f, :] += o_f
            o_ref[rows_b, :] += o_b
        if bcd:
            bcd_convs(i)
        return s_f, s_b

    half = n_chunks // 2
    states = lax.fori_loop(0, half, functools.partial(scan, first_visit=True), (s0f_ref[...], s0b_ref[...]))
    s_f, s_b = lax.fori_loop(half, n_chunks, functools.partial(scan, first_visit=False), states)
    sf_ref[...] = s_f
    sb_ref[...] = s_b

    def finish(c, carry):
        rows = pl.ds(pl.multiple_of(c * W, W), W)
        o = o_ref[rows, :]
        ms = head_sums([o * o])[0] * (1.0 / HEAD_DIM)
        y_ref[rows, :W] = (o * lax.rsqrt(ms + EPS) * ng_ref[...]).astype(y_ref.dtype)
        if bcd:
            for n in range(W // MLP_CHUNK):
                bcd_smlp(c * (W // MLP_CHUNK) + n)
        return carry

    lax.fori_loop(0, seq // W, finish, 0)


def _mixer(p_qkv, p_ab, conv_w, a_log_l, dt_bias_l, gdn_norm_g_l, s0_f, s0_b, bcd_args=None):
    bsz, seq, _ = p_qkv.shape
    W = BRANCH_W
    n_chunks = seq // GDN_CHUNK
    interleave = min(GDN_INTERLEAVE, n_chunks)
    bcd = bcd_args is not None
    pad8 = jnp.zeros((AB_PAD - 2 * N_HEADS,), F32)
    alog_row = jnp.concatenate([a_log_l.reshape(-1), pad8]).reshape(1, AB_PAD)
    dtb_row = jnp.concatenate([dt_bias_l.reshape(-1), pad8]).reshape(1, AB_PAD)
    ng_row = jnp.tile(gdn_norm_g_l, N_HEADS).reshape(1, W)
    per_b = lambda r, wd: pl.BlockSpec((None, r, wd), lambda b: (b, 0, 0))
    const = lambda r, wd: pl.BlockSpec((r, wd), lambda b: (0, 0))
    in_specs = [per_b(seq, 3 * W), per_b(seq, AB_PAD), const(3, 3 * W), const(1, AB_PAD),
                const(1, AB_PAD), const(1, W), per_b(W, W), per_b(W, W)]
    args = [p_qkv, p_ab, conv_w, alog_row, dtb_row, ng_row, s0_f, s0_b]
    scratch = [
        pltpu.VMEM((2, seq, W), BF16),
        pltpu.VMEM((2, seq, W), BF16),
        pltpu.VMEM((2, seq, W), BF16),
        pltpu.VMEM((2, seq, W), BF16),
        pltpu.VMEM((2, seq, W), BF16),
        pltpu.VMEM((2, n_chunks * SUBLANES, W), F32),
        pltpu.VMEM((seq, W), F32),
        pltpu.VMEM((2, interleave, GDN_CHUNK, 3 * W), F32),
        pltpu.VMEM((2, interleave, GDN_CHUNK, 4 * W), F32),
    ]
    if bcd:
        p_b, p_cd, short_w, conf_w, conf_b, conf_g, conf_beta, smlp_g, smlp_beta, smlp_w, smlp_b = bcd_args
        row = lambda a: a.reshape(1, W)
        ws_rows = smlp_w.reshape(N_HEADS * MLP_CHUNK, MLP_CHUNK)
        bs_exp = jnp.repeat(smlp_b.T, HEAD_DIM, axis=1)
        in_specs += [per_b(seq, 3 * W), per_b(seq, 4 * W), const(3, W), const(CONF_CONV, W)] + [const(1, W)] * 5
        in_specs += [const(N_HEADS * MLP_CHUNK, MLP_CHUNK), const(MLP_CHUNK, W)]
        args += [p_b, p_cd, short_w, conf_w, row(conf_b), row(conf_g), row(conf_beta), row(smlp_g),
                 row(smlp_beta), ws_rows, bs_exp]
        assert len(args) == 8 + N_BCD_INPUTS
        scratch += [pltpu.VMEM((seq + 2 * CONV_PAD, W), F32),
                    pltpu.VMEM((seq, W), BF16)]
    y_w = 4 * W if bcd else W
    return pl.pallas_call(
        functools.partial(_mixer_kernel, bcd=bcd),
        grid=(bsz,),
        in_specs=in_specs,
        out_specs=[per_b(seq, y_w), per_b(W, W), per_b(W, W)],
        out_shape=[jax.ShapeDtypeStruct((bsz, seq, y_w), BF16),
                   jax.ShapeDtypeStruct((bsz, W, W), F32),
                   jax.ShapeDtypeStruct((bsz, W, W), F32)],
        scratch_shapes=scratch,
        compiler_params=pltpu.CompilerParams(
            dimension_semantics=("arbitrary",), vmem_limit_bytes=VMEM_LIMIT),
        name="mixer",
    )(*args)


def _layernorm(x, g, b):
    mu = jnp.mean(x, axis=-1, keepdims=True)
    xc = x - mu
    var = jnp.mean(xc * xc, axis=-1, keepdims=True)
    return xc * lax.rsqrt(var + EPS) * g + b


N_BCD_INPUTS = 11


def _bcd_stages(pb_ref, pcd_ref, sw_ref, cw_ref, cb_ref, clg_ref, clb_ref, slg_ref, slb_ref,
                ws_ref, bs_ref, y_ref, z_ref, vln_ref):
    seq = pb_ref.shape[0]
    W = BRANCH_W
    T = GDN_CHUNK
    n_tiles = seq // T
    row = lax.broadcasted_iota(jnp.int32, (T, W), 0)
    sw = sw_ref[...]
    cw = cw_ref[...]

    z_ref[0:CONV_PAD, :] = jnp.zeros((CONV_PAD, W), F32)
    z_ref[CONV_PAD + seq:, :] = jnp.zeros((CONV_PAD, W), F32)

    def stage(t, dep):
        r0 = pl.multiple_of(t * T, T)
        cd = pcd_ref[pl.ds(r0, T), :].astype(F32) + dep
        z_ref[pl.ds(pl.multiple_of(r0 + CONV_PAD, SUBLANES), T), :] = cd[:, :W] * _sigmoid(cd[:, W:2 * W])
        vln_ref[pl.ds(r0, T), :] = _layernorm(cd[:, 3 * W:], slg_ref[...], slb_ref[...]).astype(vln_ref.dtype)

    def convs(t):
        r0 = pl.multiple_of(t * T, T)
        pb = pb_ref[pl.ds(r0, T), :].astype(F32)
        cur = pb[:, W:2 * W] * pb[:, 2 * W:]
        pv = pb_ref[pl.ds(pl.multiple_of(jnp.maximum(r0 - ROW_PACK, 0), ROW_PACK), ROW_PACK), :].astype(F32)
        nx = pb_ref[pl.ds(pl.multiple_of(jnp.minimum(r0 + T, seq - ROW_PACK), ROW_PACK), ROW_PACK), :].astype(F32)
        prev_row = (pv[:, W:2 * W] * pv[:, 2 * W:])[ROW_PACK - 1:ROW_PACK, :] * jnp.where(t > 0, 1.0, 0.0)
        next_row = (nx[:, W:2 * W] * nx[:, 2 * W:])[0:1, :] * jnp.where(t < n_tiles - 1, 1.0, 0.0)
        up = jnp.where(row == 0, prev_row, pltpu.roll(cur, 1, 0))
        dn = jnp.where(row == T - 1, next_row, pltpu.roll(cur, T - 1, 0))
        y_ref[pl.ds(r0, T), W:2 * W] = (pb[:, :W] * (up * sw[0:1, :] + cur * sw[1:2, :] + dn * sw[2:3, :])
                                        ).astype(y_ref.dtype)
        acc = jnp.zeros((T, W), F32)
        for r in range(SUBLANES):
            part = None
            for a in range(4):
                o = SUBLANES * a + r
                if o < 1 or o > CONF_CONV:
                    continue
                win = z_ref[pl.ds(pl.multiple_of(r0 + SUBLANES * a, SUBLANES), T + SUBLANES), :]
                term = win * cw[o - 1:o, :]
                part = term if part is None else part + term
            if r:
                part = pltpu.roll(part, T + SUBLANES - r, 0)
            acc = acc + part[:T, :]
        zc = _layernorm(acc + cb_ref[...], clg_ref[...], clb_ref[...])
        y_ref[pl.ds(r0, T), 2 * W:3 * W] = _silu(zc).astype(y_ref.dtype)

    lane_grp = lax.broadcasted_iota(jnp.int32, (MLP_CHUNK, W), 1) >> 6
    wsb = ws_ref[...].astype(BF16)

    def smlp(n):
        r0 = pl.multiple_of(n * MLP_CHUNK, MLP_CHUNK)
        full = jnp.dot(wsb, vln_ref[pl.ds(r0, MLP_CHUNK), :].astype(BF16), preferred_element_type=F32)
        mixed = bs_ref[...]
        for g in range(N_HEADS):
            mixed = mixed + jnp.where(lane_grp == g, full[g * MLP_CHUNK:(g + 1) * MLP_CHUNK, :], 0.0)
        u = pcd_ref[pl.ds(r0, MLP_CHUNK), 2 * W:3 * W].astype(F32)
        y_ref[pl.ds(r0, MLP_CHUNK), 3 * W:] = (u * mixed).astype(y_ref.dtype)

    return stage, convs, smlp


def _split_w_in(w_in):
    b0 = A_COLS
    c0 = b0 + B_COLS
    g0 = c0 + C_COLS + D_COLS
    w_ab = jnp.pad(w_in[:, :, 3 * BRANCH_W:A_COLS], ((0, 0), (0, 0), (0, AB_PAD - 4 * N_HEADS)))
    groups = [w_in[:, :, :3 * BRANCH_W], w_in[:, :, b0:c0], w_in[:, :, c0:g0], w_in[:, :, g0:], w_ab]
    return [g.astype(BF16) for g in groups]


def kernel(x, c, ctx, c_ctx, norm_g, w_ada, b_ada, w_in, qkv_conv_w, a_log, dt_bias, gdn_norm_g,
           short_conv_w, conf_conv_w, conf_conv_b, conf_ln_g, conf_ln_b, smlp_ln_g, smlp_ln_b,
           smlp_w, smlp_b, w_out, final_g):
    bsz, seq, _ = x.shape
    depth = w_in.shape[0]
    assert bsz + 1 <= MOD_ROWS - 7 and seq % min(TOKEN_TILE, seq) == 0

    cvec = jnp.zeros((MOD_ROWS, D_MODEL), F32).at[:bsz].set(c).at[MOD_ROWS - 8].set(c_ctx)
    mod3 = _modulation(cvec, w_ada, b_ada).reshape(depth * MOD_ROWS, 1, 3 * D_MODEL)
    w_groups = _split_w_in(w_in)
    w_out_b = w_out.astype(BF16)
    s_zero = jnp.zeros((bsz, BRANCH_W, BRANCH_W), F32)

    xc = ctx
    for l in range(depth):
        last = l == depth - 1
        col_major = l % 2 == 1
        mix_args = (short_conv_w[l], conf_conv_w[l], conf_conv_b[l], conf_ln_g[l], conf_ln_b[l],
                    smlp_ln_g[l], smlp_ln_b[l], smlp_w[l], smlp_b[l])
        if last:
            pc_qkv, pc_ab = _in_projection(xc, mod3, norm_g[l], [w_groups[0], w_groups[-1]], l, False, True)
        else:
            pc_qkv, pc_b, pc_cd, pc_gate, pc_ab = _in_projection(xc, mod3, norm_g[l], w_groups, l, False, True)
        gdn_args = (qkv_conv_w[l], a_log[l], dt_bias[l], gdn_norm_g[l])
        yc, s_f, s_b = _mixer(pc_qkv, pc_ab, *gdn_args, s_zero, s_zero,
                              None if last else (pc_b, pc_cd) + mix_args)
        p_qkv, p_b, p_cd, p_gate, p_ab = _in_projection(x, mod3, norm_g[l], w_groups, l, col_major, False)
        y, _, _ = _mixer(p_qkv, p_ab, *gdn_args, s_f, s_b, (p_b, p_cd) + mix_args)
        x = _out_projection(y, p_gate, w_out_b, mod3, x, l, col_major, False, final_g, last)
        if not last:
            xc = _out_projection(yc, pc_gate, w_out_b, mod3, xc, l, False, True, final_g, False)
    return x
```

```python
import functools

import jax
import jax.numpy as jnp
from jax import lax
from jax.experimental import pallas as pl
from jax.experimental.pallas import tpu as pltpu

F32 = jnp.float32
BF16 = jnp.bfloat16

D_MODEL = 1024
DEPTH = 4
GRID_W = 64
HEAD_DIM = 64
BRANCH_W = 256
N_HEADS = 4
A_COLS = 3 * BRANCH_W + 4 * N_HEADS
B_COLS = 3 * BRANCH_W
C_COLS = 2 * BRANCH_W
D_COLS = 2 * BRANCH_W
IN_COLS = A_COLS + B_COLS + C_COLS + D_COLS + D_MODEL
CONF_CONV = 31
GDN_CHUNK = 64
MLP_CHUNK = 128
EPS = 1e-6

LANES = 128
SUBLANES = 8
ROW_PACK = 16
AB_PAD = LANES
MOD_ROWS = 24
TOKEN_TILE = 1024
PERM_COLS = 8
CONV_PAD = 16
VMEM_LIMIT = 58 * 1024 * 1024
NEG_BIG = -1e30
INV_BASE = 32
GDN_INTERLEAVE = 4


def _sigmoid(x):
    return 1.0 / (1.0 + jnp.exp(-x))


def _silu(x):
    return x * _sigmoid(x)


def _softplus(x):
    return jnp.maximum(x, 0.0) + jnp.log1p(jnp.exp(-jnp.abs(x)))


def _dot(a, b):
    return jnp.dot(a.astype(BF16), b.astype(BF16), preferred_element_type=F32)


def _dot_nt(a, b):
    return lax.dot_general(a.astype(BF16), b.astype(BF16), (((1,), (1,)), ((), ())),
                           preferred_element_type=F32)


def _dot_tn(a, b):
    return lax.dot_general(a.astype(BF16), b.astype(BF16), (((0,), (0,)), ((), ())),
                           preferred_element_type=F32)


def _split3(x):
    x1 = x.astype(BF16)
    r1 = x - x1.astype(F32)
    x2 = r1.astype(BF16)
    x3 = (r1 - x2.astype(F32)).astype(BF16)
    return x1, x2, x3


def _mod_kernel(c_ref, w_ref, b_ref, o_ref):
    s = _silu(c_ref[...])
    o_ref[...] = jnp.dot(s, w_ref[...], preferred_element_type=F32) + b_ref[...]


def _modulation(cvec, w_ada, b_ada):
    depth = w_ada.shape[0]
    return pl.pallas_call(
        _mod_kernel,
        grid=(depth, 3),
        in_specs=[
            pl.BlockSpec((MOD_ROWS, D_MODEL), lambda l, k: (0, 0)),
            pl.BlockSpec((None, D_MODEL, D_MODEL), lambda l, k: (l, 0, k)),
            pl.BlockSpec((None, 1, D_MODEL), lambda l, k: (l, 0, k)),
        ],
        out_specs=pl.BlockSpec((None, MOD_ROWS, D_MODEL), lambda l, k: (l, 0, k)),
        out_shape=jax.ShapeDtypeStruct((depth, MOD_ROWS, 3 * D_MODEL), F32),
        compiler_params=pltpu.CompilerParams(
            dimension_semantics=("arbitrary", "arbitrary"), vmem_limit_bytes=VMEM_LIMIT),
        name="modulation",
    )(cvec, w_ada, b_ada.reshape(depth, 1, 3 * D_MODEL))


def _to_sequence_order(n_rows, n_cols, inverse=False):
    i = lax.broadcasted_iota(jnp.int32, (n_rows * n_cols, n_rows * n_cols), 0)
    j = lax.broadcasted_iota(jnp.int32, (n_rows * n_cols, n_rows * n_cols), 1)
    if inverse:
        i, j = j, i
    return j == (i % n_rows) * n_cols + i // n_rows


def _column_groups(ref):
    return [(slice(None), slice(g * PERM_COLS, (g + 1) * PERM_COLS), slice(None))
            for g in range(ref.shape[1] // PERM_COLS)]


def _inproj_kernel(x_ref, sh_ref, sc_ref, g_ref, *refs):
    w_refs, out_refs = refs[:len(refs) // 2], refs[len(refs) // 2:]

    def modulated(x):
        ms = jnp.mean(x * x, axis=-1, keepdims=True)
        h = x * lax.rsqrt(ms + EPS) * g_ref[...]
        return (h * (1.0 + sc_ref[...]) + sh_ref[...]).astype(BF16)

    if len(x_ref.shape) == 3:
        rows = x_ref.shape[0]
        perm = _to_sequence_order(rows, PERM_COLS).astype(BF16)
        hb = jnp.concatenate(
            [jnp.dot(perm, modulated(x_ref[idx].reshape(rows * PERM_COLS, D_MODEL)),
                     preferred_element_type=F32).astype(BF16) for idx in _column_groups(x_ref)], axis=0)
    else:
        hb = modulated(x_ref[...])
    for w_ref, o_ref in zip(w_refs, out_refs):
        o_ref[...] = jnp.dot(hb, w_ref[...], preferred_element_type=F32).astype(o_ref.dtype)


def _mod_spec(layer, kind, ctx):
    if ctx:
        return pl.BlockSpec((None, 1, D_MODEL), lambda b, t: (layer * MOD_ROWS + MOD_ROWS - 8, 0, kind))
    return pl.BlockSpec((None, 1, D_MODEL), lambda b, t: (layer * MOD_ROWS + b, 0, kind))


def _token_spec(bsz, seq, width, col_major):
    tile = min(TOKEN_TILE, seq)
    if not col_major:
        return tile, (bsz, seq, width), pl.BlockSpec((None, tile, width), lambda b, t: (b, t, 0))
    rows = seq // GRID_W
    assert tile % (rows * PERM_COLS) == 0
    return (tile, (bsz, rows, GRID_W, width),
            pl.BlockSpec((None, rows, tile // rows, width), lambda b, t: (b, 0, t, 0)))


def _layer_weight_spec(w, layer):
    return pl.BlockSpec((None,) + w.shape[1:], lambda b, t: (layer, 0, 0))


def _in_projection(x, mod3, norm_g_l, ws, layer, col_major, ctx):
    bsz, seq, _ = x.shape
    tile, view, x_spec = _token_spec(bsz, seq, D_MODEL, col_major)
    widths = [w.shape[2] for w in ws]
    return pl.pallas_call(
        _inproj_kernel,
        grid=(bsz, seq // tile),
        in_specs=[x_spec] + [
            _mod_spec(layer, 0, ctx), _mod_spec(layer, 1, ctx),
            pl.BlockSpec((1, D_MODEL), lambda b, t: (0, 0)),
        ] + [_layer_weight_spec(w, layer) for w in ws],
        out_specs=[pl.BlockSpec((None, tile, wd), lambda b, t: (b, t, 0)) for wd in widths],
        out_shape=[jax.ShapeDtypeStruct((bsz, seq, wd), F32 if wd == AB_PAD else BF16) for wd in widths],
        compiler_params=pltpu.CompilerParams(
            dimension_semantics=("arbitrary", "arbitrary"), vmem_limit_bytes=VMEM_LIMIT),
        name="in_projection",
    )(x.reshape(view), mod3, mod3, norm_g_l.reshape(1, D_MODEL), *ws)


def _outproj_kernel(y_ref, pg_ref, w_ref, gate_ref, x_ref, fg_ref, o_ref, *, final):
    def residual(x, r):
        xn = x + gate_ref[...] * r
        if final:
            ms = jnp.mean(xn * xn, axis=-1, keepdims=True)
            xn = xn * lax.rsqrt(ms + EPS) * fg_ref[...]
        return xn

    z = (y_ref[...].astype(F32) * _silu(pg_ref[...].astype(F32))).astype(BF16)
    if len(x_ref.shape) == 3:
        rows = x_ref.shape[0]
        n = rows * PERM_COLS
        unperm = _to_sequence_order(rows, PERM_COLS, inverse=True).astype(BF16)
        groups = _column_groups(x_ref)
        z = jnp.concatenate(
            [jnp.dot(unperm, z[g * n:(g + 1) * n, :], preferred_element_type=F32).astype(BF16)
             for g in range(len(groups))], axis=0)
        r = jnp.dot(z, w_ref[...], preferred_element_type=F32)
        for g, idx in enumerate(groups):
            xn = residual(x_ref[idx].reshape(n, D_MODEL), r[g * n:(g + 1) * n, :])
            o_ref[idx] = xn.reshape(rows, PERM_COLS, D_MODEL)
    else:
        o_ref[...] = residual(x_ref[...], jnp.dot(z, w_ref[...], preferred_element_type=F32))


def _out_projection(y, p_gate, w_out, mod3, x, layer, col_major, ctx, final_g, final):
    bsz, seq, _ = x.shape
    tile, view, x_spec = _token_spec(bsz, seq, D_MODEL, col_major)
    tok = pl.BlockSpec((None, tile, D_MODEL), lambda b, t: (b, t, 0))
    out = pl.pallas_call(
        functools.partial(_outproj_kernel, final=final),
        grid=(bsz, seq // tile),
        in_specs=[tok, tok,
                  _layer_weight_spec(w_out, layer),
                  _mod_spec(layer, 2, ctx), x_spec,
                  pl.BlockSpec((1, D_MODEL), lambda b, t: (0, 0))],
        out_specs=x_spec,
        out_shape=jax.ShapeDtypeStruct(view, F32),
        compiler_params=pltpu.CompilerParams(
            dimension_semantics=("arbitrary", "arbitrary"), vmem_limit_bytes=VMEM_LIMIT),
        name="out_projection",
    )(y, p_gate, w_out, mod3, x.reshape(view), final_g.reshape(1, D_MODEL))
    return out.reshape(bsz, seq, D_MODEL)


def _mixer_kernel(*refs, bcd):
    pq_ref, pab_ref, cw_ref, alog_ref, dtb_ref, ng_ref, s0f_ref, s0b_ref = refs[:8]
    n_in = 8 + (N_BCD_INPUTS if bcd else 0)
    (y_ref, sf_ref, sb_ref, u_ref, w_ref, at_ref, kd_ref, qe_ref, gl_ref, o_ref, qkv_ref,
     gx_ref) = refs[n_in:n_in + 12]
    if bcd:
        bcd_stage, bcd_convs, bcd_smlp = _bcd_stages(*refs[8:n_in], y_ref, *refs[n_in + 12:])
    seq = pq_ref.shape[0]
    n_chunks = seq // GDN_CHUNK
    C = GDN_CHUNK
    W = BRANCH_W
    NG = 4 * N_HEADS

    ri = lax.broadcasted_iota(jnp.int32, (C, W), 0)
    ci = lax.broadcasted_iota(jnp.int32, (C, W), 1)
    cj = ci & (C - 1)
    low_incl = ri >= cj
    low_strict = ri > cj
    up_incl = ri <= cj
    up_strict = ri < cj
    eye_cat = (ri == cj).astype(F32)
    same_block = lambda bits: (ri >> bits) == (cj >> bits)
    base_bits = INV_BASE.bit_length() - 1
    base_block = same_block(base_bits)
    merge_levels = [same_block(b + 1) & ~same_block(b) for b in range(base_bits, GDN_CHUNK.bit_length() - 1)]
    r2 = lax.broadcasted_iota(jnp.int32, (W, W), 0)
    c2 = lax.broadcasted_iota(jnp.int32, (W, W), 1)
    blockmask = (r2 >> 6) == (c2 >> 6)
    ones_blk = blockmask.astype(BF16)
    e_r = lax.broadcasted_iota(jnp.int32, (NG, 4 * W), 0)
    e_c = lax.broadcasted_iota(jnp.int32, (NG, 4 * W), 1)
    expand = (e_r == ((e_c >> 8) * N_HEADS + ((e_c & (W - 1)) >> 6))).astype(BF16)
    lane = lax.broadcasted_iota(jnp.int32, (C, AB_PAD), 1)
    rown = lax.broadcasted_iota(jnp.int32, (C, AB_PAD), 0)
    row64 = lax.broadcasted_iota(jnp.int32, (C, 3 * W), 0)

    def tile_heads(x):
        return jnp.concatenate([x] * N_HEADS, axis=0)

    def bd(x):
        return jnp.where(blockmask, tile_heads(x), 0.0)

    def cat_mm(a_cat, b_cat):
        return _dot(a_cat, bd(b_cat))

    def head_sums(xs):
        r = jnp.dot(jnp.concatenate([x.astype(BF16) for x in xs], axis=0), ones_blk, preferred_element_type=F32)
        m = xs[0].shape[0]
        return [r[i * m:(i + 1) * m] for i in range(len(xs))]

    def chunk_cumsum(x, reverse):
        s = 1
        while s < C:
            if reverse:
                x = x + jnp.where(rown < C - s, pltpu.roll(x, C - s, 0), 0.0)
            else:
                x = x + jnp.where(rown >= s, pltpu.roll(x, s, 0), 0.0)
            s *= 2
        return x

    cw = cw_ref[...]

    def conv_act(c, dep):
        r0 = pl.multiple_of(c * C, C)
        cur = pq_ref[pl.ds(r0, C), :].astype(F32) + dep
        prev = pq_ref[pl.ds(pl.multiple_of(jnp.maximum(r0 - ROW_PACK, 0), ROW_PACK), ROW_PACK), :].astype(F32)
        nxt = pq_ref[pl.ds(pl.multiple_of(jnp.minimum(r0 + C, seq - ROW_PACK), ROW_PACK), ROW_PACK), :].astype(F32)
        prev_row = prev[ROW_PACK - 1:ROW_PACK, :] * jnp.where(c > 0, 1.0, 0.0)
        next_row = nxt[0:1, :] * jnp.where(c < n_chunks - 1, 1.0, 0.0)
        up = jnp.where(row64 == 0, prev_row, pltpu.roll(cur, 1, 0))
        dn = jnp.where(row64 == C - 1, next_row, pltpu.roll(cur, C - 1, 0))
        return _silu(up * cw[0:1, :] + cur * cw[1:2, :] + dn * cw[2:3, :])

    def gate_sums(c, dep):
        ab = pab_ref[pl.ds(pl.multiple_of(c * C, C), C), :] + dep
        gk = -jnp.exp(alog_ref[...]) * _softplus(ab + dtb_ref[...])
        gb = jnp.where(lane < 2 * N_HEADS, gk, _sigmoid(ab))
        nar = jnp.where(lane < N_HEADS, chunk_cumsum(gb, False),
                        jnp.where(lane < 2 * N_HEADS, chunk_cumsum(gb, True), gb))
        return _split3(nar[:, :NG])

    G = qkv_ref.shape[1]
    n_iter = n_chunks // G

    def front_end(i, slot):
        cs = [i * G + j for j in range(G)]
        acts, nars = [], []
        pieces = [functools.partial(lambda c, dep: acts.append(conv_act(c, dep)), c) for c in cs]
        pieces += [functools.partial(lambda c, dep: nars.append(gate_sums(c, dep)), c) for c in cs]
        if bcd:
            pieces += [functools.partial(bcd_stage, c) for c in cs]

        def finish():
            sums = [head_sums([a[:, :W] * a[:, :W], a[:, W:2 * W] * a[:, W:2 * W]]) for a in acts]
            for j, (a, s, n3) in enumerate(zip(acts, sums, nars)):
                qkv_ref[slot, j, :, :W] = a[:, :W] * lax.rsqrt(s[0] + EPS) * (HEAD_DIM ** -0.5)
                qkv_ref[slot, j, :, W:2 * W] = a[:, W:2 * W] * lax.rsqrt(s[1] + EPS)
                qkv_ref[slot, j, :, 2 * W:] = a[:, 2 * W:]
                gx_ref[slot, j] = sum(jnp.dot(n, expand, preferred_element_type=F32) for n in n3)

        return pieces, finish

    def prepare(i, slot):
        pieces, finish = front_end(i, slot)
        for piece in pieces:
            piece(0.0)
        finish()

    def parallel_step(i, slot, prepare_next):
        pieces, finish = front_end(jnp.minimum(i + 1, n_iter - 1), 1 - slot) if prepare_next else ([], None)

        def fill(stage_out):
            if pieces:
                pieces.pop(0)(stage_out[0][0:1, 0:1] * 0.0)

        cs = [i * G + j for j in range(G)]
        chains = [(j, d) for j in range(G) for d in range(2)]
        q = [qkv_ref[slot, j, :, :W] for j in range(G)]
        k = [qkv_ref[slot, j, :, W:2 * W] for j in range(G)]
        v = [qkv_ref[slot, j, :, 2 * W:] for j in range(G)]
        gx = [gx_ref[slot, j] for j in range(G)]
        prod = [_dot_nt(jnp.concatenate([q[j], k[j]], axis=0), bd(k[j])) for j in range(G)]
        fill(prod)
        gi = [gx[j][:, d * W:(d + 1) * W] for j, d in chains]
        beta = [gx[j][:, (2 + d) * W:(3 + d) * W] for j, d in chains]
        incl = [up_incl if d else low_incl for _, d in chains]
        strict = [up_strict if d else low_strict for _, d in chains]
        g_end = [g[0:1, :] if d else g[C - 1:C, :] for g, (_, d) in zip(gi, chains)]
        gj = [jnp.sum(g * eye_cat, axis=0, keepdims=True) for g in gi]
        decay = [jnp.exp(jnp.where(m, g - g2, NEG_BIG)) for m, g, g2 in zip(incl, gi, gj)]
        eg = [jnp.exp(g) for g in gi]
        lmat = [-jnp.where(m, prod[j][C:] * b * dc, 0.0)
                for m, (j, _), b, dc in zip(strict, chains, beta, decay)]
        lblk = [jnp.where(base_block, l, 0.0) for l in lmat]
        tinv = [eye_cat + l for l in lblk]
        power = [cat_mm(l, l) for l in lblk]
        fill(power)
        for _ in range(base_bits - 2):
            st = [cat_mm(jnp.concatenate([t, p], axis=0), p) for t, p in zip(tinv, power)]
            tinv = [t + s[:C] for t, s in zip(tinv, st)]
            power = [s[C:] for s in st]
            fill(st)
        st = [cat_mm(t, p) for t, p in zip(tinv, power)]
        tinv = [t + s for t, s in zip(tinv, st)]
        fill(st)
        for level in merge_levels:
            y = [cat_mm(jnp.where(level, l, 0.0), t) for l, t in zip(lmat, tinv)]
            fill(y)
            st = [cat_mm(t, x) for t, x in zip(tinv, y)]
            tinv = [t + s for t, s in zip(tinv, st)]
            fill(st)
        tinv = [t.astype(BF16) for t in tinv]
        u = [jnp.dot(t, bd(v[j] * b).astype(BF16), preferred_element_type=F32)
             for t, (j, _), b in zip(tinv, chains, beta)]
        fill(u)
        w = [jnp.dot(t, bd(k[j] * b * e).astype(BF16), preferred_element_type=F32)
             for t, (j, _), b, e in zip(tinv, chains, beta, eg)]
        fill(w)
        for n, (j, d) in enumerate(chains):
            rows = pl.ds(pl.multiple_of(cs[j] * C, C), C)
            u_ref[d, rows, :] = u[n].astype(BF16)
            w_ref[d, rows, :] = (-w[n]).astype(BF16)
            at_ref[d, rows, :] = jnp.where(incl[n], prod[j][:C] * decay[n], 0.0).astype(BF16)
            kd_ref[d, rows, :] = (k[j] * jnp.exp(g_end[n] - gi[n])).astype(BF16)
            qe_ref[d, rows, :] = (q[j] * eg[n]).astype(BF16)
            gl_ref[d, pl.ds(pl.multiple_of(cs[j] * SUBLANES, SUBLANES), SUBLANES), :] = jnp.broadcast_to(
                jnp.exp(g_end[n]), (SUBLANES, W))
        if prepare_next:
            while pieces:
                fill(w)
            finish()

    def parallel_pair(p, carry):
        parallel_step(2 * p, 0, True)
        parallel_step(2 * p + 1, 1, True)
        return carry

    prepare(0, 0)
    lax.fori_loop(0, n_iter // 2, parallel_pair, 0)
    if n_iter % 2:
        parallel_step(n_iter - 1, 0, False)

    def state_free(d, c):
        rows = pl.ds(pl.multiple_of(c * C, C), C)
        u, w, at, kd = u_ref[d, rows, :], w_ref[d, rows, :], at_ref[d, rows, :], kd_ref[d, rows, :]
        au = jnp.dot(at, bd(u), preferred_element_type=F32)
        aw = jnp.dot(at, bd(w), preferred_element_type=F32)
        n_all = _dot_tn(kd, u)
        m_all = _dot_tn(kd, w)
        lhs = jnp.concatenate([(qe_ref[d, rows, :].astype(F32) + aw).astype(BF16), m_all.astype(BF16)], axis=0)
        gl = gl_ref[d, pl.ds(pl.multiple_of(c * SUBLANES, SUBLANES), 1), :]
        return lhs, au, n_all, gl

    def state_step(parts, s):
        lhs, au, n_all, gl = parts
        t = jnp.dot(lhs, s.astype(BF16), preferred_element_type=F32)
        return t[:C] + au, s * gl + jnp.where(blockmask, t[C:] + n_all, 0.0)

    def scan(i, carry, first_visit):
        s_f, s_b = carry
        cb = n_chunks - 1 - i
        parts_f, parts_b = state_free(0, i), state_free(1, cb)
        o_f, s_f = state_step(parts_f, s_f)
        o_b, s_b = state_step(parts_b, s_b)
        rows_f = pl.ds(pl.multiple_of(i * C, C), C)
        rows_b = pl.ds(pl.multiple_of(cb * C, C), C)
        if first_visit:
            o_ref[rows_f, :] = o_f
            o_ref[rows_b, :] = o_b
        else:
            o_ref[rows_f, :] += o_f
            o_ref[rows_b, :] += o_b
        if bcd:
            bcd_convs(i)
        return s_f, s_b

    half = n_chunks // 2
    states = lax.fori_loop(0, half, functools.partial(scan, first_visit=True), (s0f_ref[...], s0b_ref[...]))
    s_f, s_b = lax.fori_loop(half, n_chunks, functools.partial(scan, first_visit=False), states)
    sf_ref[...] = s_f
    sb_ref[...] = s_b

    def finish(c, carry):
        rows = pl.ds(pl.multiple_of(c * W, W), W)
        o = o_ref[rows, :]
        ms = head_sums([o * o])[0] * (1.0 / HEAD_DIM)
        y_ref[rows, :W] = (o * lax.rsqrt(ms + EPS) * ng_ref[...]).astype(y_ref.dtype)
        if bcd:
            for n in range(W // MLP_CHUNK):
                bcd_smlp(c * (W // MLP_CHUNK) + n)
        return carry

    lax.fori_loop(0, seq // W, finish, 0)


def _mixer(p_qkv, p_ab, conv_w, a_log_l, dt_bias_l, gdn_norm_g_l, s0_f, s0_b, bcd_args=None):
    bsz, seq, _ = p_qkv.shape
    W = BRANCH_W
    n_chunks = seq // GDN_CHUNK
    interleave = min(GDN_INTERLEAVE, n_chunks)
    bcd = bcd_args is not None
    pad8 = jnp.zeros((AB_PAD - 2 * N_HEADS,), F32)
    alog_row = jnp.concatenate([a_log_l.reshape(-1), pad8]).reshape(1, AB_PAD)
    dtb_row = jnp.concatenate([dt_bias_l.reshape(-1), pad8]).reshape(1, AB_PAD)
    ng_row = jnp.tile(gdn_norm_g_l, N_HEADS).reshape(1, W)
    per_b = lambda r, wd: pl.BlockSpec((None, r, wd), lambda b: (b, 0, 0))
    const = lambda r, wd: pl.BlockSpec((r, wd), lambda b: (0, 0))
    in_specs = [per_b(seq, 3 * W), per_b(seq, AB_PAD), const(3, 3 * W), const(1, AB_PAD),
                const(1, AB_PAD), const(1, W), per_b(W, W), per_b(W, W)]
    args = [p_qkv, p_ab, conv_w, alog_row, dtb_row, ng_row, s0_f, s0_b]
    scratch = [
        pltpu.VMEM((2, seq, W), BF16),
        pltpu.VMEM((2, seq, W), BF16),
        pltpu.VMEM((2, seq, W), BF16),
        pltpu.VMEM((2, seq, W), BF16),
        pltpu.VMEM((2, seq, W), BF16),
        pltpu.VMEM((2, n_chunks * SUBLANES, W), F32),
        pltpu.VMEM((seq, W), F32),
        pltpu.VMEM((2, interleave, GDN_CHUNK, 3 * W), F32),
        pltpu.VMEM((2, interleave, GDN_CHUNK, 4 * W), F32),
    ]
    if bcd:
        p_b, p_cd, short_w, conf_w, conf_b, conf_g, conf_beta, smlp_g, smlp_beta, smlp_w, smlp_b = bcd_args
        row = lambda a: a.reshape(1, W)
        ws_rows = smlp_w.reshape(N_HEADS * MLP_CHUNK, MLP_CHUNK)
        bs_exp = jnp.repeat(smlp_b.T, HEAD_DIM, axis=1)
        in_specs += [per_b(seq, 3 * W), per_b(seq, 4 * W), const(3, W), const(CONF_CONV, W)] + [const(1, W)] * 5
        in_specs += [const(N_HEADS * MLP_CHUNK, MLP_CHUNK), const(MLP_CHUNK, W)]
        args += [p_b, p_cd, short_w, conf_w, row(conf_b), row(conf_g), row(conf_beta), row(smlp_g),
                 row(smlp_beta), ws_rows, bs_exp]
        assert len(args) == 8 + N_BCD_INPUTS
        scratch += [pltpu.VMEM((seq + 2 * CONV_PAD, W), F32),
                    pltpu.VMEM((seq, W), BF16)]
    y_w = 4 * W if bcd else W
    return pl.pallas_call(
        functools.partial(_mixer_kernel, bcd=bcd),
        grid=(bsz,),
        in_specs=in_specs,
        out_specs=[per_b(seq, y_w), per_b(W, W), per_b(W, W)],
        out_shape=[jax.ShapeDtypeStruct((bsz, seq, y_w), BF16),
                   jax.ShapeDtypeStruct((bsz, W, W), F32),
                   jax.ShapeDtypeStruct((bsz, W, W), F32)],
        scratch_shapes=scratch,
        compiler_params=pltpu.CompilerParams(
            dimension_semantics=("arbitrary",), vmem_limit_bytes=VMEM_LIMIT),
        name="mixer",
    )(*args)


def _layernorm(x, g, b):
    mu = jnp.mean(x, axis=-1, keepdims=True)
    xc = x - mu
    var = jnp.mean(xc * xc, axis=-1, keepdims=True)
    return xc * lax.rsqrt(var + EPS) * g + b


N_BCD_INPUTS = 11


def _bcd_stages(pb_ref, pcd_ref, sw_ref, cw_ref, cb_ref, clg_ref, clb_ref, slg_ref, slb_ref,
                ws_ref, bs_ref, y_ref, z_ref, vln_ref):
    seq = pb_ref.shape[0]
    W = BRANCH_W
    T = GDN_CHUNK
    n_tiles = seq // T
    row = lax.broadcasted_iota(jnp.int32, (T, W), 0)
    sw = sw_ref[...]
    cw = cw_ref[...]

    z_ref[0:CONV_PAD, :] = jnp.zeros((CONV_PAD, W), F32)
    z_ref[CONV_PAD + seq:, :] = jnp.zeros((CONV_PAD, W), F32)

    def stage(t, dep):
        r0 = pl.multiple_of(t * T, T)
        cd = pcd_ref[pl.ds(r0, T), :].astype(F32) + dep
        z_ref[pl.ds(pl.multiple_of(r0 + CONV_PAD, SUBLANES), T), :] = cd[:, :W] * _sigmoid(cd[:, W:2 * W])
        vln_ref[pl.ds(r0, T), :] = _layernorm(cd[:, 3 * W:], slg_ref[...], slb_ref[...]).astype(vln_ref.dtype)

    def convs(t):
        r0 = pl.multiple_of(t * T, T)
        pb = pb_ref[pl.ds(r0, T), :].astype(F32)
        cur = pb[:, W:2 * W] * pb[:, 2 * W:]
        pv = pb_ref[pl.ds(pl.multiple_of(jnp.maximum(r0 - ROW_PACK, 0), ROW_PACK), ROW_PACK), :].astype(F32)
        nx = pb_ref[pl.ds(pl.multiple_of(jnp.minimum(r0 + T, seq - ROW_PACK), ROW_PACK), ROW_PACK), :].astype(F32)
        prev_row = (pv[:, W:2 * W] * pv[:, 2 * W:])[ROW_PACK - 1:ROW_PACK, :] * jnp.where(t > 0, 1.0, 0.0)
        next_row = (nx[:, W:2 * W] * nx[:, 2 * W:])[0:1, :] * jnp.where(t < n_tiles - 1, 1.0, 0.0)
        up = jnp.where(row == 0, prev_row, pltpu.roll(cur, 1, 0))
        dn = jnp.where(row == T - 1, next_row, pltpu.roll(cur, T - 1, 0))
        y_ref[pl.ds(r0, T), W:2 * W] = (pb[:, :W] * (up * sw[0:1, :] + cur * sw[1:2, :] + dn * sw[2:3, :])
                                        ).astype(y_ref.dtype)
        acc = jnp.zeros((T, W), F32)
        for r in range(SUBLANES):
            part = None
            for a in range(4):
                o = SUBLANES * a + r
                if o < 1 or o > CONF_CONV:
                    continue
                win = z_ref[pl.ds(pl.multiple_of(r0 + SUBLANES * a, SUBLANES), T + SUBLANES), :]
                term = win * cw[o - 1:o, :]
                part = term if part is None else part + term
            if r:
                part = pltpu.roll(part, T + SUBLANES - r, 0)
            acc = acc + part[:T, :]
        zc = _layernorm(acc + cb_ref[...], clg_ref[...], clb_ref[...])
        y_ref[pl.ds(r0, T), 2 * W:3 * W] = _silu(zc).astype(y_ref.dtype)

    lane_grp = lax.broadcasted_iota(jnp.int32, (MLP_CHUNK, W), 1) >> 6
    wsb = ws_ref[...].astype(BF16)

    def smlp(n):
        r0 = pl.multiple_of(n * MLP_CHUNK, MLP_CHUNK)
        full = jnp.dot(wsb, vln_ref[pl.ds(r0, MLP_CHUNK), :].astype(BF16), preferred_element_type=F32)
        mixed = bs_ref[...]
        for g in range(N_HEADS):
            mixed = mixed + jnp.where(lane_grp == g, full[g * MLP_CHUNK:(g + 1) * MLP_CHUNK, :], 0.0)
        u = pcd_ref[pl.ds(r0, MLP_CHUNK), 2 * W:3 * W].astype(F32)
        y_ref[pl.ds(r0, MLP_CHUNK), 3 * W:] = (u * mixed).astype(y_ref.dtype)

    return stage, convs, smlp


def _split_w_in(w_in):
    b0 = A_COLS
    c0 = b0 + B_COLS
    g0 = c0 + C_COLS + D_COLS
    w_ab = jnp.pad(w_in[:, :, 3 * BRANCH_W:A_COLS], ((0, 0), (0, 0), (0, AB_PAD - 4 * N_HEADS)))
    groups = [w_in[:, :, :3 * BRANCH_W], w_in[:, :, b0:c0], w_in[:, :, c0:g0], w_in[:, :, g0:], w_ab]
    return [g.astype(BF16) for g in groups]


def kernel(x, c, ctx, c_ctx, norm_g, w_ada, b_ada, w_in, qkv_conv_w, a_log, dt_bias, gdn_norm_g,
           short_conv_w, conf_conv_w, conf_conv_b, conf_ln_g, conf_ln_b, smlp_ln_g, smlp_ln_b,
           smlp_w, smlp_b, w_out, final_g):
    bsz, seq, _ = x.shape
    depth = w_in.shape[0]
    assert bsz + 1 <= MOD_ROWS - 7 and seq % min(TOKEN_TILE, seq) == 0

    cvec = jnp.zeros((MOD_ROWS, D_MODEL), F32).at[:bsz].set(c).at[MOD_ROWS - 8].set(c_ctx)
    mod3 = _modulation(cvec, w_ada, b_ada).reshape(depth * MOD_ROWS, 1, 3 * D_MODEL)
    w_groups = _split_w_in(w_in)
    w_out_b = w_out.astype(BF16)
    s_zero = jnp.zeros((bsz, BRANCH_W, BRANCH_W), F32)

    xc = ctx
    for l in range(depth):
        last = l == depth - 1
        col_major = l % 2 == 1
        mix_args = (short_conv_w[l], conf_conv_w[l], conf_conv_b[l], conf_ln_g[l], conf_ln_b[l],
                    smlp_ln_g[l], smlp_ln_b[l], smlp_w[l], smlp_b[l])
        if last:
            pc_qkv, pc_ab = _in_projection(xc, mod3, norm_g[l], [w_groups[0], w_groups[-1]], l, False, True)
        else:
            pc_qkv, pc_b, pc_cd, pc_gate, pc_ab = _in_projection(xc, mod3, norm_g[l], w_groups, l, False, True)
        gdn_args = (qkv_conv_w[l], a_log[l], dt_bias[l], gdn_norm_g[l])
        yc, s_f, s_b = _mixer(pc_qkv, pc_ab, *gdn_args, s_zero, s_zero,
                              None if last else (pc_b, pc_cd) + mix_args)
        p_qkv, p_b, p_cd, p_gate, p_ab = _in_projection(x, mod3, norm_g[l], w_groups, l, col_major, False)
        y, _, _ = _mixer(p_qkv, p_ab, *gdn_args, s_f, s_b, (p_b, p_cd) + mix_args)
        x = _out_projection(y, p_gate, w_out_b, mod3, x, l, col_major, False, final_g, last)
        if not last:
            xc = _out_projection(yc, pc_gate, w_out_b, mod3, xc, l, False, True, final_g, False)
    return x
```
